```python
import math, functools
import jax, jax.numpy as jnp
from jax import lax
import numpy as np

D_MODEL = 2048
BATCH = 8
SEQ = 2048
DEPTH = 1
DEC_BATCH = 128
DEC_SEQ = 1
PAST_LEN = 16384
PAGE_SIZE = 128

HEAD_DIM = 128
N_HEADS = D_MODEL // HEAD_DIM
DIFF_HEADS = N_HEADS // 2
MLA_HEADS = N_HEADS - DIFF_HEADS
DIFF_KV_HEADS = 2
DIFF_GROUP = DIFF_HEADS // DIFF_KV_HEADS
DIFF_DH = HEAD_DIM // 2
DIFF_VD = HEAD_DIM
Q_LORA = 512
KV_LORA = 512
MLA_NOPE = 128
MLA_ROPE = 64
MLA_QK_DIM = MLA_NOPE + MLA_ROPE
MLA_VD = HEAD_DIM
ROPE_THETA = 10000.0
D_MIX = DIFF_HEADS * DIFF_VD + MLA_HEADS * MLA_VD
DIFF_Q_COLS = DIFF_HEADS * 2 * DIFF_DH
DIFF_K_COLS = DIFF_KV_HEADS * 2 * DIFF_DH
DIFF_V_COLS = DIFF_KV_HEADS * DIFF_VD
MLA_QA_COLS = Q_LORA
MLA_KVA_COLS = KV_LORA + MLA_ROPE
IN_COLS = DIFF_Q_COLS + DIFF_K_COLS + DIFF_V_COLS + MLA_QA_COLS + MLA_KVA_COLS
IN_SPLITS = (DIFF_Q_COLS,
             DIFF_Q_COLS + DIFF_K_COLS,
             DIFF_Q_COLS + DIFF_K_COLS + DIFF_V_COLS,
             DIFF_Q_COLS + DIFF_K_COLS + DIFF_V_COLS + MLA_QA_COLS)
N_EXPERTS = 32
TOP_K = 4
D_FF = D_MODEL
SWIGLU_LIMIT = 7.0
SWIGLU_ALPHA = 1.702
MOE_BLOCK = 128
Q_BLOCK = 128
RMS_EPS = 1e-6
NEG_INF = -1e30
F32 = jnp.float32

kernel_name = 'hymba_diffattn_mla_moe_adaln_step'


def rms_norm(x, g):
    xf = x.astype(F32)
    y = xf * lax.rsqrt(jnp.mean(xf * xf, axis=-1, keepdims=True) + RMS_EPS)
    return (y * g.astype(F32)).astype(x.dtype)


def alibi_slopes(n):
    return 2.0 ** (-8.0 * jnp.arange(1, n + 1, dtype=F32) / n)


def rope_tables(pos):
    inv = ROPE_THETA ** (-jnp.arange(0, MLA_ROPE, 2, dtype=F32) / MLA_ROPE)
    ang = pos.astype(F32)[:, None] * inv[None, :]
    return jnp.cos(ang), jnp.sin(ang)


def apply_rope(x, cos, sin):
    x1, x2 = jnp.split(x.astype(F32), 2, axis=-1)
    return jnp.concatenate([x1 * cos - x2 * sin, x1 * sin + x2 * cos], axis=-1).astype(x.dtype)


def _modulation(c, lp):
    m = jax.nn.silu(c) @ lp['w_ada'] + lp['b_ada']
    return jnp.split(m[:, None, :], 6, axis=-1)


def _project(h, pos, lp):
    B, T, _ = h.shape
    z = h @ lp['w_in']
    dq, dk, dv, mq, mkv = jnp.split(z, IN_SPLITS, axis=-1)
    qd = rms_norm(dq.reshape(B, T, DIFF_KV_HEADS, DIFF_GROUP, 2, DIFF_DH), lp['diff_qn_g']) * (DIFF_DH ** -0.5)
    kd = rms_norm(dk.reshape(B, T, DIFF_KV_HEADS, 2, DIFF_DH), lp['diff_kn_g'])
    vd = dv.reshape(B, T, DIFF_KV_HEADS, DIFF_VD)
    cos, sin = rope_tables(pos)
    qm = (rms_norm(mq, lp['mla_qa_g']) @ lp['w_qb']).reshape(B, T, MLA_HEADS, MLA_QK_DIM)
    qm = rms_norm(qm, lp['mla_qn_g']) * (MLA_QK_DIM ** -0.5)
    q_nope, q_pe = jnp.split(qm, [MLA_NOPE], axis=-1)
    q_pe = apply_rope(q_pe, cos[:, None, :], sin[:, None, :])
    q_lat = jnp.einsum('bthn,hcn->bthc', q_nope, lp['w_uk'])
    ckv = rms_norm(mkv[..., :KV_LORA], lp['mla_kva_g'])
    kpe = apply_rope(rms_norm(mkv[..., KV_LORA:], lp['mla_kpe_g']), cos, sin)
    return (qd, q_lat, q_pe), (kd, vd, ckv, kpe)


def _diff_attention(q, q_pos, segs, slopes, lam):
    scores = []
    for k, _, k_pos in segs:
        s = jnp.einsum('btkgmd,bskmd->bkgmts', q, k).astype(F32)
        dist = (q_pos[:, None] - k_pos[None, :]).astype(F32)
        s = jnp.where(dist >= 0, s - slopes[:, :, None, None, None] * dist, NEG_INF)
        scores.append(s)
    p = jax.nn.softmax(jnp.concatenate(scores, axis=-1), axis=-1)
    pd = p[:, :, :, 0] - lam * p[:, :, :, 1]
    outs, start = [], 0
    for _, v, k_pos in segs:
        n = k_pos.shape[0]
        outs.append(jnp.einsum('bkgts,bskd->btkgd', pd[..., start:start + n].astype(v.dtype), v))
        start += n
    return functools.reduce(jnp.add, outs)


def _mla_attention(q_lat, q_pe, q_pos, segs):
    scores = []
    for ckv, kpe, k_pos in segs:
        s = (jnp.einsum('bthc,bsc->bhts', q_lat, ckv) + jnp.einsum('bthr,bsr->bhts', q_pe, kpe)).astype(F32)
        scores.append(jnp.where(q_pos[:, None] >= k_pos[None, :], s, NEG_INF))
    p = jax.nn.softmax(jnp.concatenate(scores, axis=-1), axis=-1)
    outs, start = [], 0
    for ckv, _, k_pos in segs:
        n = k_pos.shape[0]
        outs.append(jnp.einsum('bhts,bsc->bthc', p[..., start:start + n].astype(ckv.dtype), ckv))
        start += n
    return functools.reduce(jnp.add, outs)


def _mix_heads(qd, q_lat, q_pe, q_pos, segs, lp, lam, lam_init, slopes):
    B, T = qd.shape[:2]
    od = _diff_attention(qd, q_pos, [(s[0], s[1], s[4]) for s in segs], slopes, lam)
    od = rms_norm(od.reshape(B, T, DIFF_HEADS, DIFF_VD), lp['diff_subln_g']) * (1.0 - lam_init)
    o_lat = _mla_attention(q_lat, q_pe, q_pos, [(s[2], s[3], s[4]) for s in segs])
    om = jnp.einsum('bthc,hcv->bthv', o_lat, lp['w_uv'])
    return jnp.concatenate([od.reshape(B, T, -1), om.reshape(B, T, -1)], axis=-1)


def _prompt_attend(qs, kvs, pos, mix):
    kd, vd, ckv, kpe = kvs
    segs = [(kd, vd, ckv, kpe, pos)]
    B, T = kd.shape[:2]
    nb = T // Q_BLOCK

    def to_blocks(a):
        return jnp.moveaxis(a.reshape((B, nb, Q_BLOCK) + a.shape[2:]), 1, 0)

    def one_block(args):
        qd_b, ql_b, qp_b, pos_b = args
        return mix(qd_b, ql_b, qp_b, pos_b, segs)

    out = lax.map(one_block, tuple(to_blocks(a) for a in qs) + (pos.reshape(nb, Q_BLOCK),))
    return jnp.moveaxis(out, 0, 1).reshape(B, T, -1)


def _sample_attend(qs, kvs, pos, past_pos, page_table, caches, layer, mix):
    cdk, cdv, cckv, ckpe = caches

    def one_seq(args):
        qd_i, ql_i, qp_i, kd_i, vd_i, ckv_i, kpe_i, pt_i = args

        def past(cache):
            g = cache[layer, pt_i]
            return g.reshape((1, -1) + g.shape[2:])

        segs = [(past(cdk), past(cdv), past(cckv), past(ckpe), past_pos),
                (kd_i[None], vd_i[None], ckv_i[None], kpe_i[None], pos)]
        return mix(qd_i[None], ql_i[None], qp_i[None], pos, segs)[0]

    return lax.map(one_seq, qs + kvs + (page_table,))


def _moe(h, lp):
    n, d = h.shape
    logits = h.astype(F32) @ lp['w_router'].astype(F32) + lp['b_router'].astype(F32)
    top_val, top_idx = lax.top_k(logits, TOP_K)
    gates = jax.nn.softmax(top_val, axis=-1)
    flat_e = top_idx.reshape(-1)
    flat_tok = jnp.repeat(jnp.arange(n, dtype=jnp.int32), TOP_K)
    flat_gate = gates.reshape(-1)
    order = jnp.argsort(flat_e)
    e_sorted = flat_e[order]
    counts = jnp.bincount(flat_e, length=N_EXPERTS)
    padded = (counts + MOE_BLOCK - 1) // MOE_BLOCK * MOE_BLOCK
    pad_end = jnp.cumsum(padded)
    pad_start = pad_end - padded
    grp_start = jnp.cumsum(counts) - counts
    dest = pad_start[e_sorted] + jnp.arange(n * TOP_K) - grp_start[e_sorted]
    n_blocks = -(-(n * TOP_K + N_EXPERTS * (MOE_BLOCK - 1)) // MOE_BLOCK)
    n_rows = n_blocks * MOE_BLOCK
    row_tok = jnp.zeros((n_rows,), jnp.int32).at[dest].set(flat_tok[order])
    row_gate = jnp.zeros((n_rows,), F32).at[dest].set(flat_gate[order])
    block_exp = jnp.minimum(jnp.searchsorted(pad_end, jnp.arange(n_blocks) * MOE_BLOCK, side='right'), N_EXPERTS - 1)
    xb = h[row_tok].reshape(n_blocks, MOE_BLOCK, d)

    def expert_block(args):
        xk, e = args
        gu = xk @ lp['w_gu'][e] + lp['b_gu'][e]
        gate, up = jnp.split(gu, 2, axis=-1)
        gate = jnp.minimum(gate, SWIGLU_LIMIT)
        up = jnp.clip(up, -SWIGLU_LIMIT, SWIGLU_LIMIT)
        act = (up + 1.0) * gate * jax.nn.sigmoid(SWIGLU_ALPHA * gate)
        return act @ lp['w_down'][e] + lp['b_down'][e]

    yb = lax.map(expert_block, (xb, block_exp)).reshape(n_rows, d)
    return jnp.zeros_like(h).at[row_tok].add(yb * row_gate[:, None].astype(h.dtype))


def _layer(x, c, pos, attend, lp):
    sh1, sc1, gt1, sh2, sc2, gt2 = _modulation(c, lp)
    h = rms_norm(x, lp['norm1_g']) * (1.0 + sc1) + sh1
    qs, kvs = _project(h, pos, lp)
    x = x + gt1 * (attend(qs, kvs) @ lp['w_out'])
    h = rms_norm(x, lp['norm2_g']) * (1.0 + sc2) + sh2
    B, T, D = x.shape
    x = x + gt2 * _moe(h.reshape(B * T, D), lp).reshape(B, T, D)
    return x, kvs


def setup_inputs(seed: int = 0) -> dict:
    key = jax.random.key(seed)
    ks = jax.random.split(key, 32)

    def nrm(i, shape, scale):
        return jax.random.normal(ks[i], shape, F32) * scale

    def gain(i, shape):
        return 1.0 + 0.05 * jax.random.normal(ks[i], shape, F32)

    n_pages = PAST_LEN // PAGE_SIZE
    n_pool = (DEC_BATCH * n_pages * 5) // 4
    page_table = jax.random.permutation(ks[0], n_pool)[: DEC_BATCH * n_pages].reshape(DEC_BATCH, n_pages).astype(jnp.int32)
    D = D_MODEL
    return {
        'x_prompt': nrm(1, (BATCH, SEQ, D), 1.0),
        'x_sample': nrm(2, (DEC_BATCH, DEC_SEQ, D), 1.0),
        'cache_dk': nrm(3, (DEPTH, n_pool, PAGE_SIZE, DIFF_KV_HEADS, 2, DIFF_DH), 1.0),
        'cache_dv': nrm(4, (DEPTH, n_pool, PAGE_SIZE, DIFF_KV_HEADS, DIFF_VD), 1.0),
        'cache_ckv': nrm(5, (DEPTH, n_pool, PAGE_SIZE, KV_LORA), 1.0),
        'cache_kpe': nrm(6, (DEPTH, n_pool, PAGE_SIZE, MLA_ROPE), 1.0),
        'page_table': page_table,
        'c_prompt': nrm(7, (BATCH, D), 1.0),
        'c_sample': nrm(8, (DEC_BATCH, D), 1.0),
        'norm1_g': gain(9, (DEPTH, D)),
        'norm2_g': gain(10, (DEPTH, D)),
        'w_ada': nrm(11, (DEPTH, D, 6 * D), 0.3 * D ** -0.5),
        'b_ada': nrm(12, (DEPTH, 6 * D), 0.02),
        'w_in': nrm(13, (DEPTH, D, IN_COLS), D ** -0.5),
        'diff_qn_g': gain(14, (DEPTH, 2, DIFF_DH)),
        'diff_kn_g': gain(15, (DEPTH, 2, DIFF_DH)),
        'diff_lambda': nrm(16, (DEPTH, 4, DIFF_DH), 0.1),
        'diff_subln_g': gain(17, (DEPTH, DIFF_VD)),
        'mla_qa_g': gain(18, (DEPTH, Q_LORA)),
        'w_qb': nrm(19, (DEPTH, Q_LORA, MLA_HEADS * MLA_QK_DIM), Q_LORA ** -0.5),
        'mla_qn_g': gain(20, (DEPTH, MLA_QK_DIM)),
        'mla_kva_g': gain(21, (DEPTH, KV_LORA)),
        'mla_kpe_g': gain(22, (DEPTH, MLA_ROPE)),
        'w_uk': nrm(23, (DEPTH, MLA_HEADS, KV_LORA, MLA_NOPE), KV_LORA ** -0.5),
        'w_uv': nrm(24, (DEPTH, MLA_HEADS, KV_LORA, MLA_VD), KV_LORA ** -0.5),
        'w_out': nrm(25, (DEPTH, D_MIX, D), D_MIX ** -0.5),
        'w_router': nrm(26, (DEPTH, D, N_EXPERTS), D ** -0.5),
        'b_router': nrm(27, (DEPTH, N_EXPERTS), 0.01),
        'w_gu': nrm(28, (DEPTH, N_EXPERTS, D, 2 * D_FF), D ** -0.5),
        'b_gu': nrm(29, (DEPTH, N_EXPERTS, 2 * D_FF), 0.02),
        'w_down': nrm(30, (DEPTH, N_EXPERTS, D_FF, D), D_FF ** -0.5),
        'b_down': nrm(31, (DEPTH, N_EXPERTS, D), 0.02),
    }


def reference(x_prompt, x_sample, cache_dk, cache_dv, cache_ckv, cache_kpe, page_table, c_prompt, c_sample,
              norm1_g, norm2_g, w_ada, b_ada, w_in, diff_qn_g, diff_kn_g, diff_lambda, diff_subln_g,
              mla_qa_g, w_qb, mla_qn_g, mla_kva_g, mla_kpe_g, w_uk, w_uv, w_out,
              w_router, b_router, w_gu, b_gu, w_down, b_down):
    weights = {'norm1_g': norm1_g, 'norm2_g': norm2_g, 'w_ada': w_ada, 'b_ada': b_ada, 'w_in': w_in,
               'diff_qn_g': diff_qn_g, 'diff_kn_g': diff_kn_g, 'diff_lambda': diff_lambda,
               'diff_subln_g': diff_subln_g, 'mla_qa_g': mla_qa_g, 'w_qb': w_qb, 'mla_qn_g': mla_qn_g,
               'mla_kva_g': mla_kva_g, 'mla_kpe_g': mla_kpe_g, 'w_uk': w_uk, 'w_uv': w_uv, 'w_out': w_out,
               'w_router': w_router, 'b_router': b_router, 'w_gu': w_gu, 'b_gu': b_gu,
               'w_down': w_down, 'b_down': b_down}
    pos_p = jnp.arange(x_prompt.shape[1], dtype=jnp.int32)
    past_len = page_table.shape[1] * PAGE_SIZE
    past_pos = jnp.arange(past_len, dtype=jnp.int32)
    pos_s = past_len + jnp.arange(x_sample.shape[1], dtype=jnp.int32)
    slopes = alibi_slopes(DIFF_HEADS).reshape(DIFF_KV_HEADS, DIFF_GROUP)
    caches = (cache_dk, cache_dv, cache_ckv, cache_kpe)
    y_p, y_s = x_prompt, x_sample
    rows_p, rows_s = [], []
    for layer in range(DEPTH):
        lp = {name: w[layer] for name, w in weights.items()}
        lam_init = 0.8 - 0.6 * math.exp(-0.3 * layer)
        lv = lp['diff_lambda'].astype(F32)
        lam = jnp.exp(jnp.dot(lv[0], lv[1])) - jnp.exp(jnp.dot(lv[2], lv[3])) + lam_init
        mix = functools.partial(_mix_heads, lp=lp, lam=lam, lam_init=lam_init, slopes=slopes)
        attend_p = functools.partial(_prompt_attend, pos=pos_p, mix=mix)
        attend_s = functools.partial(_sample_attend, pos=pos_s, past_pos=past_pos, page_table=page_table,
                                     caches=caches, layer=layer, mix=mix)
        y_p, kv_p = _layer(y_p, c_prompt, pos_p, attend_p, lp)
        y_s, kv_s = _layer(y_s, c_sample, pos_s, attend_s, lp)
        rows_p.append(kv_p)
        rows_s.append(kv_s)
    dk_p, dv_p, ckv_p, kpe_p = (jnp.stack([r[i] for r in rows_p]) for i in range(4))
    dk_s, dv_s, ckv_s, kpe_s = (jnp.stack([r[i] for r in rows_s]) for i in range(4))
    return (y_p, y_s, dk_p, dv_p, ckv_p, kpe_p, dk_s, dv_s, ckv_s, kpe_s)
```

```python
import functools
import math

import jax
import jax.numpy as jnp
from jax import lax
from jax.experimental import pallas as pl
from jax.experimental.pallas import tpu as pltpu

F32 = jnp.float32
BF16 = jnp.bfloat16
U32 = jnp.uint32
I32 = jnp.int32

LANES = 128
V7X_VMEM_BYTES = 64 * 1024 * 1024

HEAD_DIM = 128
DIFF_KV_HEADS = 2
DIFF_GROUP = 4
DIFF_HEADS = DIFF_KV_HEADS * DIFF_GROUP
DIFF_DH = 64
MLA_HEADS = 8
Q_LORA = 512
KV_LORA = 512
MLA_NOPE = 128
MLA_ROPE = 64
MLA_QK_DIM = MLA_NOPE + MLA_ROPE
ROPE_THETA = 10000.0
N_EXPERTS = 32
TOP_K = 4
SWIGLU_LIMIT = 7.0
SWIGLU_ALPHA = 1.702
PAGE_SIZE = 128
RMS_EPS = 1e-6
NEG_INF = -1e30

DIFF_Q_COLS = DIFF_HEADS * 2 * DIFF_DH
DIFF_K_COLS = DIFF_KV_HEADS * 2 * DIFF_DH
DIFF_V_COLS = DIFF_KV_HEADS * HEAD_DIM
O_DQ = 0
O_DK = O_DQ + DIFF_Q_COLS
O_DV = O_DK + DIFF_K_COLS
O_MQ = O_DV + DIFF_V_COLS
O_CKV = O_MQ + Q_LORA
O_KPE = O_CKV + KV_LORA
IN_COLS = O_KPE + MLA_ROPE
IN_COLS_EXT = IN_COLS + MLA_ROPE

TM_PROJ = 256
TQ_ATTN = 256
TK_ATTN = 256
TM_TOK = 128
TB_MOE = 256
TN_GU = 512
TN_DOWN = 512
TN_ADA = 1024
PAGES_PER_CHUNK = 16


def _vmem_limit(nbytes):
    return int(min(max(nbytes * 5 // 4, 16 * 1024 * 1024), V7X_VMEM_BYTES - 8 * 1024 * 1024))


def _cparams(sem, vmem_bytes):
    return pltpu.CompilerParams(dimension_semantics=sem, vmem_limit_bytes=_vmem_limit(vmem_bytes))


def _dot(a, b):
    return jnp.dot(a, b, preferred_element_type=F32)


def _dot_nt(a, b):
    return lax.dot_general(a, b, (((1,), (1,)), ((), ())), preferred_element_type=F32)


def _lane_iota(shape):
    return lax.broadcasted_iota(I32, shape, len(shape) - 1)


def _mod_kernel(c_ref, w_ref, b_ref, o_ref):
    c = c_ref[...]
    a = (c * jax.nn.sigmoid(c)).astype(BF16)
    o_ref[...] = _dot(a, w_ref[...].astype(BF16)) + b_ref[...]


def _modulation(c_all, w_ada, b_ada):
    r, d = c_all.shape
    n = w_ada.shape[1]
    tn = TN_ADA
    return pl.pallas_call(
        _mod_kernel,
        out_shape=jax.ShapeDtypeStruct((r, n), F32),
        grid=(n // tn,),
        in_specs=[pl.BlockSpec((r, d), lambda j: (0, 0)),
                  pl.BlockSpec((d, tn), lambda j: (0, j)),
                  pl.BlockSpec((1, tn), lambda j: (0, j))],
        out_specs=pl.BlockSpec((r, tn), lambda j: (0, j)),
        compiler_params=_cparams(("arbitrary",), 2 * d * tn * 4 + d * tn * 2 + 4 * r * d * 4),
        name="adaln_modulation",
    )(c_all, w_ada, b_ada.reshape(1, n))


def _group64_sumsq(x, gmat):
    x2 = (x * x).astype(BF16)
    outs = [_dot(x2[:, j * LANES:(j + 1) * LANES], gmat) for j in range(x.shape[1] // LANES)]
    return outs[0] if len(outs) == 1 else jnp.concatenate(outs, axis=1)


def _rope_pair_block(blk, tab):
    r = blk * tab
    return r + pltpu.roll(r, 64, 1)


def _proj_kernel(x_ref, sh_ref, sc_ref, tab_ref, g1_ref, win_ref, gmat_ref, gdq_ref, gdk_ref, gqa_ref,
                 gckv_ref, gkpe_ref, wqb_ref, gqn_ref, gqp_ref, wup_ref, *out_refs, sample):
    x = x_ref[...]
    h = x * lax.rsqrt(jnp.mean(x * x, axis=-1, keepdims=True) + RMS_EPS) * g1_ref[...]
    h = h * (1.0 + sc_ref[...]) + sh_ref[...]
    z = _dot(h.astype(BF16), win_ref[...])
    gmat = gmat_ref[...]
    tab = tab_ref[...]
    lane = _lane_iota((1, LANES))
    low_half = lane < 64

    dq = z[:, O_DQ:O_DK]
    qd = dq * lax.rsqrt(_group64_sumsq(dq, gmat) * (1.0 / DIFF_DH) + RMS_EPS) * gdq_ref[...]
    dk = z[:, O_DK:O_DV]
    kd = dk * lax.rsqrt(_group64_sumsq(dk, gmat) * (1.0 / DIFF_DH) + RMS_EPS) * gdk_ref[...]
    vd = z[:, O_DV:O_MQ]
    mq = z[:, O_MQ:O_CKV]
    ckv = z[:, O_CKV:O_KPE]
    kpb = z[:, O_KPE:IN_COLS_EXT]

    ckv_n = ckv * lax.rsqrt(jnp.mean(ckv * ckv, axis=-1, keepdims=True) + RMS_EPS) * gckv_ref[...]
    kp_ms = 0.5 * jnp.sum(kpb * kpb, axis=-1, keepdims=True) * (1.0 / MLA_ROPE)
    kpn = kpb * lax.rsqrt(kp_ms + RMS_EPS) * gkpe_ref[...]
    kr = _rope_pair_block(kpn, tab)

    mqn = (mq * lax.rsqrt(jnp.mean(mq * mq, axis=-1, keepdims=True) + RMS_EPS) * gqa_ref[...]).astype(BF16)
    qm = _dot(mqn, wqb_ref[...])
    gqn = gqn_ref[...]
    gqp = gqp_ref[...]
    q_nope, q_rope = [], []
    for hh in range(MLA_HEADS):
        n_h = qm[:, hh * LANES:(hh + 1) * LANES]
        p_h = qm[:, (MLA_HEADS + hh) * LANES:(MLA_HEADS + hh + 1) * LANES]
        ms = (jnp.sum(n_h * n_h, axis=-1, keepdims=True)
              + 0.5 * jnp.sum(p_h * p_h, axis=-1, keepdims=True)) * (1.0 / MLA_QK_DIM)
        r = lax.rsqrt(ms + RMS_EPS)
        q_nope.append(n_h * r * gqn)
        q_rope.append(_rope_pair_block(p_h * r * gqp, tab))

    if sample:
        qd_ref, dk_ref, dv_ref, ckv_ref, kpe_ref, qlat_ref, qpe_ref = out_refs
        qd_ref[...] = qd
        qlat_ref[...] = jnp.concatenate(
            [_dot(q_nope[hh].astype(BF16), wup_ref[hh]) for hh in range(MLA_HEADS)], axis=1)
        qpe_ref[...] = jnp.concatenate(q_rope, axis=1)
    else:
        qds_ref, dk_ref, dv_ref, ckv_ref, kpe_ref, qmla_ref, kmla_ref, vmla_ref = out_refs
        blocks = []
        for c in range(DIFF_HEADS):
            blk = qd[:, c * LANES:(c + 1) * LANES]
            blocks.append(jnp.where(low_half, blk, 0.0))
            blocks.append(jnp.where(low_half, 0.0, blk))
        qds_ref[...] = jnp.concatenate(blocks, axis=1).astype(BF16)
        kvup = _dot(ckv_n.astype(BF16), wup_ref[...])
        kr_lo = jnp.where(low_half, kr, 0.0)
        qb, kb = [], []
        for hh in range(MLA_HEADS):
            qb += [q_nope[hh], q_rope[hh]]
            kb += [kvup[:, hh * LANES:(hh + 1) * LANES], kr_lo]
        qmla_ref[...] = jnp.concatenate(qb, axis=1).astype(BF16)
        kmla_ref[...] = jnp.concatenate(kb, axis=1).astype(BF16)
        vmla_ref[...] = kvup[:, MLA_HEADS * LANES:].astype(BF16)
    dk_ref[...] = kd
    dv_ref[...] = vd
    ckv_ref[...] = ckv_n
    kpe_ref[...] = kr[:, :MLA_ROPE]


def _project(x2d, shift, scale, tab, consts, wup, *, sample, rows_per_mod, tm):
    n, d = x2d.shape
    steps = n // tm
    mod_rows = shift.shape[1]
    tab_rows = tab.shape[0]
    tab_block = tm if tab_rows > 1 else 1
    tab_steps = max(tab_rows // tm, 1)
    mod_spec = pl.BlockSpec((None, mod_rows, d), lambda i: (i * tm // rows_per_mod, 0, 0))
    tab_spec = pl.BlockSpec((tab_block, LANES), lambda i: (i % tab_steps, 0))

    def full(a):
        nd = a.ndim
        return pl.BlockSpec(a.shape, lambda i: (0,) * nd)

    def rows(cols, dt):
        return jax.ShapeDtypeStruct((n, cols), dt), pl.BlockSpec((tm, cols), lambda i: (i, 0))

    if sample:
        outs = [rows(DIFF_Q_COLS, F32), rows(DIFF_K_COLS, F32), rows(DIFF_V_COLS, F32), rows(KV_LORA, F32),
                rows(MLA_ROPE, F32), rows(MLA_HEADS * KV_LORA, F32), rows(MLA_HEADS * LANES, F32)]
    else:
        outs = [rows(2 * DIFF_Q_COLS, BF16), rows(DIFF_K_COLS, F32), rows(DIFF_V_COLS, F32), rows(KV_LORA, F32),
                rows(MLA_ROPE, F32), rows(2 * MLA_HEADS * LANES, BF16), rows(2 * MLA_HEADS * LANES, BF16),
                rows(MLA_HEADS * LANES, BF16)]
    out_shape = [o[0] for o in outs]
    out_specs = [o[1] for o in outs]
    in_specs = [pl.BlockSpec((tm, d), lambda i: (i, 0)), mod_spec, mod_spec, tab_spec] + [full(a) for a in consts] + [full(wup)]
    weight_bytes = sum(int(a.size) * a.dtype.itemsize for a in consts) + int(wup.size) * wup.dtype.itemsize
    vmem = 2 * weight_bytes + 2 * tm * d * 4 + 12 * tm * IN_COLS_EXT * 4
    return pl.pallas_call(
        functools.partial(_proj_kernel, sample=sample),
        out_shape=out_shape, grid=(steps,), in_specs=in_specs, out_specs=out_specs,
        compiler_params=_cparams(("arbitrary",), vmem),
        name="in_proj_sample" if sample else "in_proj_prompt",
    )(x2d, shift, scale, tab, *consts, wup)


def _softmax_step(s, v_bf, m, l, acc):
    m_new = jnp.maximum(m, jnp.max(s, axis=-1, keepdims=True))
    alpha = jnp.exp(m - m_new)
    p = jnp.exp(s - m_new)
    l_new = alpha * l + jnp.sum(p, axis=-1, keepdims=True)
    acc_new = alpha * acc + _dot(p.astype(BF16), v_bf)
    return m_new, l_new, acc_new


def _diff_lambda(lv, lam_init):
    a = jnp.sum(lv[0:1, :] * lv[1:2, :], axis=-1, keepdims=True)
    b = jnp.sum(lv[2:3, :] * lv[3:4, :], axis=-1, keepdims=True)
    return jnp.exp(a) - jnp.exp(b) + lam_init


def _diff_prompt_kernel(lam_ref, gsub_ref, q_ref, k_ref, v_ref, o_ref, *, tq, tk, lam_init):
    kvh = pl.program_id(1)
    i = pl.program_id(2)
    q = q_ref[...]
    nrow = 2 * DIFF_GROUP
    qs = jnp.concatenate([q[:, c * LANES:(c + 1) * LANES] for c in range(nrow)], axis=0)
    kv_scale = jnp.where(kvh == 0, 1.0, 2.0 ** (-DIFF_GROUP)).astype(F32)
    rel = (lax.broadcasted_iota(I32, (tq, tk), 0) - lax.broadcasted_iota(I32, (tq, tk), 1))

    def step(j, carry, masked):
        m, l, acc = carry
        start = pl.multiple_of(j * tk, tk)
        k = k_ref[pl.ds(start, tk), :].astype(BF16)
        v = v_ref[pl.ds(start, tk), :].astype(BF16)
        s = _dot_nt(qs, k)
        dist = rel + (i * tq - j * tk)
        distf = dist.astype(F32)
        pieces = []
        for g in range(DIFF_GROUP):
            bias = (2.0 ** (-(g + 1))) * kv_scale * distf
            for mp in range(2):
                r0 = (g * 2 + mp) * tq
                sg = s[r0:r0 + tq] - bias
                if masked:
                    sg = jnp.where(dist >= 0, sg, NEG_INF)
                pieces.append(sg)
        s = jnp.concatenate(pieces, axis=0)
        return _softmax_step(s, v, m, l, acc)

    init = (jnp.full((nrow * tq, 1), NEG_INF, F32), jnp.zeros((nrow * tq, 1), F32),
            jnp.zeros((nrow * tq, HEAD_DIM), F32))
    ratio = tq // tk
    carry = lax.fori_loop(0, i * ratio, lambda j, c: step(j, c, False), init)
    for jj in range(ratio):
        carry = step(i * ratio + jj, carry, True)
    m, l, acc = carry
    o = acc / l
    lam = _diff_lambda(lam_ref[...], lam_init)
    gsub = gsub_ref[...] * (1.0 - lam_init)
    outs = []
    for g in range(DIFF_GROUP):
        og = o[(2 * g) * tq:(2 * g + 1) * tq] - lam * o[(2 * g + 1) * tq:(2 * g + 2) * tq]
        og = og * lax.rsqrt(jnp.mean(og * og, axis=-1, keepdims=True) + RMS_EPS) * gsub
        outs.append(og)
    o_ref[...] = jnp.concatenate(outs, axis=1).astype(o_ref.dtype)


def _diff_prompt_attention(qds, dk, dv, lam_v, gsub, *, batch, seq, lam_init):
    tq, tk = TQ_ATTN, TK_ATTN
    nq = seq // tq
    width = 2 * DIFF_GROUP * LANES
    vmem = 2 * (tq * width * 2 + 2 * seq * LANES * 4) + 8 * (2 * DIFF_GROUP * tq) * tk * 4
    return pl.pallas_call(
        functools.partial(_diff_prompt_kernel, tq=tq, tk=tk, lam_init=lam_init),
        out_shape=jax.ShapeDtypeStruct((batch * seq, DIFF_HEADS * HEAD_DIM), BF16),
        grid=(batch, DIFF_KV_HEADS, nq),
        in_specs=[pl.BlockSpec(lam_v.shape, lambda b, h, i: (0, 0)),
                  pl.BlockSpec(gsub.shape, lambda b, h, i: (0, 0)),
                  pl.BlockSpec((tq, width), lambda b, h, i: (b * nq + i, h)),
                  pl.BlockSpec((seq, LANES), lambda b, h, i: (b, h)),
                  pl.BlockSpec((seq, HEAD_DIM), lambda b, h, i: (b, h))],
        out_specs=pl.BlockSpec((tq, DIFF_GROUP * HEAD_DIM), lambda b, h, i: (b * nq + i, h)),
        compiler_params=_cparams(("arbitrary", "arbitrary", "arbitrary"), vmem),
        name="diff_attn_prompt",
    )(lam_v, gsub, qds, dk, dv)


def _mla_prompt_kernel(q_ref, k_ref, v_ref, o_ref, *, tq, tk):
    i = pl.program_id(2)
    q = q_ref[...]
    rel = (lax.broadcasted_iota(I32, (tq, tk), 0) - lax.broadcasted_iota(I32, (tq, tk), 1))

    def step(j, carry, masked):
        m, l, acc = carry
        start = pl.multiple_of(j * tk, tk)
        s = _dot_nt(q, k_ref[pl.ds(start, tk), :])
        if masked:
            s = jnp.where(rel + (i * tq - j * tk) >= 0, s, NEG_INF)
        return _softmax_step(s, v_ref[pl.ds(start, tk), :], m, l, acc)

    init = (jnp.full((tq, 1), NEG_INF, F32), jnp.zeros((tq, 1), F32), jnp.zeros((tq, HEAD_DIM), F32))
    ratio = tq // tk
    carry = lax.fori_loop(0, i * ratio, lambda j, c: step(j, c, False), init)
    for jj in range(ratio):
        carry = step(i * ratio + jj, carry, True)
    m, l, acc = carry
    o_ref[...] = (acc / l).astype(o_ref.dtype)


def _mla_prompt_attention(qmla, kmla, vmla, *, batch, seq):
    tq, tk = TQ_ATTN, TK_ATTN
    nq = seq // tq
    vmem = 2 * (tq * 2 * LANES * 2 + seq * 2 * LANES * 2 + seq * LANES * 2) + 8 * tq * tk * 4
    return pl.pallas_call(
        functools.partial(_mla_prompt_kernel, tq=tq, tk=tk),
        out_shape=jax.ShapeDtypeStruct((batch * seq, MLA_HEADS * HEAD_DIM), BF16),
        grid=(batch, MLA_HEADS, nq),
        in_specs=[pl.BlockSpec((tq, 2 * LANES), lambda b, h, i: (b * nq + i, h)),
                  pl.BlockSpec((seq, 2 * LANES), lambda b, h, i: (b, h)),
                  pl.BlockSpec((seq, HEAD_DIM), lambda b, h, i: (b, h))],
        out_specs=pl.BlockSpec((tq, HEAD_DIM), lambda b, h, i: (b * nq + i, h)),
        compiler_params=_cparams(("arbitrary", "arbitrary", "arbitrary"), vmem),
        name="mla_attn_prompt",
    )(qmla, kmla, vmla)


def _decode_kernel(pt_ref, lam_ref, gsub_ref, slope_ref, qlat_ref, qpe_ref, ckvn_ref, kpen_ref, qdm_ref,
                   kdn_ref, vdn_ref, c_ckv, c_kpe, c_dk, c_dv, olat_ref, od_ref,
                   ckv_buf, kpe_buf, dk_buf, dv_buf, sem, *, n_seq, n_chunks, pages, lam_init):
    s_idx = pl.program_id(0)
    chunk_len = pages * PAGE_SIZE

    def copies(seq, chunk, slot):
        out = []
        for p in range(pages):
            page = pt_ref[seq, chunk * pages + p]
            out.append(pltpu.make_async_copy(
                c_ckv.at[page], ckv_buf.at[slot, pl.ds(p * PAGE_SIZE, PAGE_SIZE), :], sem.at[slot]))
            out.append(pltpu.make_async_copy(
                c_kpe.at[page], kpe_buf.at[slot, :, pl.ds(p * PAGE_SIZE, PAGE_SIZE)], sem.at[slot]))
            out.append(pltpu.make_async_copy(
                c_dk.at[page], dk_buf.at[slot, :, pl.ds(p * PAGE_SIZE, PAGE_SIZE)], sem.at[slot]))
            out.append(pltpu.make_async_copy(
                c_dv.at[page], dv_buf.at[slot, pl.ds(2 * p * PAGE_SIZE, 2 * PAGE_SIZE), :], sem.at[slot]))
        return out

    def start_chunk(seq, chunk, slot):
        for cp in copies(seq, chunk, slot):
            cp.start()

    def wait_chunk(seq, chunk, slot):
        for cp in copies(seq, chunk, slot):
            cp.wait()

    @pl.when(s_idx == 0)
    def _():
        start_chunk(0, 0, 0)

    qlat = qlat_ref[...]
    qpe = qpe_ref[...]
    qdm = qdm_ref[...]
    qlat_bf, qpe_bf, qdm_bf = qlat.astype(BF16), qpe.astype(BF16), qdm.astype(BF16)
    ckv_new = ckvn_ref[...]
    slopes = slope_ref[...][:, 0:1]
    past_len = n_chunks * chunk_len

    m_m = (jnp.sum(qlat * ckv_new, axis=-1, keepdims=True)
           + jnp.sum(qpe * kpen_ref[...], axis=-1, keepdims=True))
    l_m = jnp.ones_like(m_m)
    acc_m = jnp.broadcast_to(ckv_new, (MLA_HEADS, KV_LORA)).astype(F32)
    m_d = jnp.sum(qdm * kdn_ref[...], axis=-1, keepdims=True)
    l_d = jnp.ones_like(m_d)
    vdn = vdn_ref[...]
    half = 2 * DIFF_GROUP
    acc_d = jnp.concatenate([jnp.broadcast_to(vdn[:, :HEAD_DIM], (half, HEAD_DIM)),
                             jnp.broadcast_to(vdn[:, HEAD_DIM:], (half, HEAD_DIM))], axis=0).astype(F32)

    def body(c, carry):
        m_m, l_m, acc_m, m_d, l_d, acc_d = carry
        step = s_idx * n_chunks + c
        slot = step % 2

        @pl.when(c + 1 < n_chunks)
        def _():
            start_chunk(s_idx, c + 1, 1 - slot)

        @pl.when(jnp.logical_and(c + 1 == n_chunks, s_idx + 1 < n_seq))
        def _():
            start_chunk(s_idx + 1, 0, 1 - slot)

        wait_chunk(s_idx, c, slot)

        ckv = ckv_buf[slot].astype(BF16)
        s_m = _dot_nt(qlat_bf, ckv) + _dot(qpe_bf, kpe_buf[slot].astype(BF16))
        m_m, l_m, acc_m = _softmax_step(s_m, ckv, m_m, l_m, acc_m)

        s_d = _dot(qdm_bf, dk_buf[slot].astype(BF16))
        pos = c * chunk_len + _lane_iota((1, chunk_len))
        s_d = s_d - slopes * (past_len - pos).astype(F32)
        m_new = jnp.maximum(m_d, jnp.max(s_d, axis=-1, keepdims=True))
        alpha = jnp.exp(m_d - m_new)
        p = jnp.exp(s_d - m_new)
        l_d = alpha * l_d + jnp.sum(p, axis=-1, keepdims=True)
        p_bf = p.astype(BF16)
        v0 = dv_buf[slot, pl.ds(0, chunk_len, stride=2), :].astype(BF16)
        v1 = dv_buf[slot, pl.ds(1, chunk_len, stride=2), :].astype(BF16)
        pv = jnp.concatenate([_dot(p_bf[:half], v0), _dot(p_bf[half:], v1)], axis=0)
        acc_d = alpha * acc_d + pv
        return m_m, l_m, acc_m, m_new, l_d, acc_d

    m_m, l_m, acc_m, m_d, l_d, acc_d = lax.fori_loop(0, n_chunks, body, (m_m, l_m, acc_m, m_d, l_d, acc_d))

    olat_ref[...] = acc_m / l_m
    o = acc_d / l_d
    lam = _diff_lambda(lam_ref[...], lam_init)
    g4 = DIFF_GROUP
    od = jnp.concatenate([o[0:g4] - lam * o[g4:2 * g4], o[2 * g4:3 * g4] - lam * o[3 * g4:4 * g4]], axis=0)
    od = od * lax.rsqrt(jnp.mean(od * od, axis=-1, keepdims=True) + RMS_EPS) * (gsub_ref[...] * (1.0 - lam_init))
    od_ref[...] = od


def _decode_attention(page_table, lam_v, gsub, slopes, qlat, qpe, ckv_new, kpe_new, qdm, kd_new, vd_new,
                      c_ckv, c_kpe, c_dk, c_dv, *, lam_init):
    n_seq, n_pages = page_table.shape
    pages = min(PAGES_PER_CHUNK, n_pages)
    n_chunks = n_pages // pages
    chunk_len = pages * PAGE_SIZE

    def per_seq(a):
        return pl.BlockSpec((None,) + a.shape[1:], lambda s, pt: (s,) + (0,) * (a.ndim - 1))

    def full(a):
        return pl.BlockSpec(a.shape, lambda s, pt: (0,) * a.ndim)

    any_spec = pl.BlockSpec(memory_space=pl.ANY)
    buf_bytes = 2 * chunk_len * (KV_LORA + MLA_ROPE + DIFF_K_COLS + DIFF_V_COLS) * 4
    grid_spec = pltpu.PrefetchScalarGridSpec(
        num_scalar_prefetch=1,
        grid=(n_seq,),
        in_specs=[full(lam_v), full(gsub), full(slopes), per_seq(qlat), per_seq(qpe), per_seq(ckv_new),
                  per_seq(kpe_new), per_seq(qdm), per_seq(kd_new), per_seq(vd_new),
                  any_spec, any_spec, any_spec, any_spec],
        out_specs=[pl.BlockSpec((None, MLA_HEADS, KV_LORA), lambda s, pt: (s, 0, 0)),
                   pl.BlockSpec((None, DIFF_HEADS, HEAD_DIM), lambda s, pt: (s, 0, 0))],
        scratch_shapes=[pltpu.VMEM((2, chunk_len, KV_LORA), F32),
                        pltpu.VMEM((2, MLA_ROPE, chunk_len), F32),
                        pltpu.VMEM((2, DIFF_K_COLS, chunk_len), F32),
                        pltpu.VMEM((2, 2 * chunk_len, HEAD_DIM), F32),
                        pltpu.SemaphoreType.DMA((2,))],
    )
    return pl.pallas_call(
        functools.partial(_decode_kernel, n_seq=n_seq, n_chunks=n_chunks, pages=pages, lam_init=lam_init),
        out_shape=[jax.ShapeDtypeStruct((n_seq, MLA_HEADS, KV_LORA), F32),
                   jax.ShapeDtypeStruct((n_seq, DIFF_HEADS, HEAD_DIM), F32)],
        grid_spec=grid_spec,
        compiler_params=_cparams(("arbitrary",), buf_bytes + buf_bytes // 2),
        name="decode_attn",
    )(page_table, lam_v, gsub, slopes, qlat, qpe, ckv_new, kpe_new, qdm, kd_new, vd_new, c_ckv, c_kpe, c_dk, c_dv)


def _latent_out_kernel(o_ref, w_ref, out_ref):
    out_ref[...] = _dot(o_ref[...].astype(BF16), w_ref[...]).astype(out_ref.dtype)


def _latent_out(olat2d, w_uv_bf):
    n = olat2d.shape[0]
    return pl.pallas_call(
        _latent_out_kernel,
        out_shape=jax.ShapeDtypeStruct((n, MLA_HEADS * HEAD_DIM), BF16),
        grid=(MLA_HEADS,),
        in_specs=[pl.BlockSpec((n, KV_LORA), lambda h: (0, h)),
                  pl.BlockSpec((None, KV_LORA, HEAD_DIM), lambda h: (h, 0, 0))],
        out_specs=pl.BlockSpec((n, HEAD_DIM), lambda h: (0, h)),
        compiler_params=_cparams(("arbitrary",), 4 * n * KV_LORA * 4),
        name="latent_out",
    )(olat2d, w_uv_bf)


def _pack_bf16_pairs(a, b):
    ua = pltpu.bitcast(a.astype(BF16).astype(F32), U32) & jnp.uint32(0xFFFF0000)
    ub = pltpu.bitcast(b.astype(BF16).astype(F32), U32) >> jnp.uint32(16)
    return ua | ub


def _unpack_bf16_pairs(u):
    a = pltpu.bitcast(u & jnp.uint32(0xFFFF0000), F32)
    b = pltpu.bitcast(u << jnp.uint32(16), F32)
    return a, b


def _outproj_kernel(ad_ref, am_ref, x_ref, gt_ref, sh_ref, sc_ref, g2_ref, wo_ref, wr_ref, br_ref, tri_ref, cin_ref,
                    x1_ref, h2_ref, slab_ref, cout_ref, carry_ref):
    i = pl.program_id(0)

    @pl.when(i == 0)
    def _():
        carry_ref[...] = cin_ref[...]

    half = wo_ref.shape[0] // 2
    mix = _dot(ad_ref[...], wo_ref[:half, :]) + _dot(am_ref[...], wo_ref[half:, :])
    x1 = x_ref[...] + gt_ref[...] * mix
    x1_ref[...] = x1
    h = x1 * lax.rsqrt(jnp.mean(x1 * x1, axis=-1, keepdims=True) + RMS_EPS) * g2_ref[...]
    h = h * (1.0 + sc_ref[...]) + sh_ref[...]
    d = h.shape[1]
    h2_ref[...] = _pack_bf16_pairs(h[:, :d // 2], h[:, d // 2:])

    logits = jnp.dot(h, wr_ref[...], preferred_element_type=F32, precision=lax.Precision.HIGHEST) + br_ref[...]
    tm, ne = logits.shape
    lane = _lane_iota((tm, ne)).astype(F32)
    slab_lane = _lane_iota((tm, LANES))
    carry = carry_ref[...]
    work = logits
    onehots, vals, idxs = [], [], []
    for _ in range(TOP_K):
        mx = jnp.max(work, axis=-1, keepdims=True)
        ix = jnp.min(jnp.where(work == mx, lane, float(ne)), axis=-1, keepdims=True)
        oh = lane == ix
        onehots.append(oh)
        vals.append(mx)
        idxs.append(ix)
        work = jnp.where(oh, -jnp.inf, work)
    cnt = sum(jnp.where(oh, 1.0, 0.0) for oh in onehots)
    before = _dot(tri_ref[...], cnt.astype(BF16)) + carry
    exps = [jnp.exp(v - vals[0]) for v in vals]
    den = sum(exps)
    slab = jnp.zeros((tm, LANES), F32)
    for k in range(TOP_K):
        rank = jnp.sum(jnp.where(onehots[k], before, 0.0), axis=-1, keepdims=True)
        slab = jnp.where(slab_lane == k, idxs[k], slab)
        slab = jnp.where(slab_lane == TOP_K + k, rank, slab)
        slab = jnp.where(slab_lane == 2 * TOP_K + k, exps[k] / den, slab)
    slab_ref[...] = slab
    new_carry = carry + jnp.sum(cnt, axis=0, keepdims=True)
    carry_ref[...] = new_carry
    cout_ref[...] = new_carry


def _out_project(attn_d, attn_m, x2d, gate, shift, scale, consts, count_in, *, rows_per_mod, tm, name):
    n, d = x2d.shape
    steps = n // tm
    mod_rows = gate.shape[1]
    mod_spec = pl.BlockSpec((None, mod_rows, d), lambda i: (i * tm // rows_per_mod, 0, 0))

    def full(a):
        nd = a.ndim
        return pl.BlockSpec(a.shape, lambda i: (0,) * nd)

    in_specs = [pl.BlockSpec((tm, d // 2), lambda i: (i, 0)), pl.BlockSpec((tm, d // 2), lambda i: (i, 0)),
                pl.BlockSpec((tm, d), lambda i: (i, 0)), mod_spec, mod_spec, mod_spec] + [full(a) for a in consts] + [full(count_in)]
    out_shape = [jax.ShapeDtypeStruct((n, d), F32), jax.ShapeDtypeStruct((n, d // 2), U32),
                 jax.ShapeDtypeStruct((n, LANES), F32), jax.ShapeDtypeStruct(count_in.shape, F32)]
    out_specs = [pl.BlockSpec((tm, d), lambda i: (i, 0)), pl.BlockSpec((tm, d // 2), lambda i: (i, 0)),
                 pl.BlockSpec((tm, LANES), lambda i: (i, 0)), full(count_in)]
    weight_bytes = sum(int(a.size) * a.dtype.itemsize for a in consts)
    return pl.pallas_call(
        _outproj_kernel, out_shape=out_shape, grid=(steps,), in_specs=in_specs, out_specs=out_specs,
        scratch_shapes=[pltpu.VMEM(count_in.shape, F32)],
        compiler_params=_cparams(("arbitrary",), 2 * weight_bytes + 16 * tm * d * 4),
        name=name,
    )(attn_d, attn_m, x2d, gate, shift, scale, *consts, count_in)


def _dispatch_kernel(dest_ref, hp_ref, hs_ref, xs_in, xs_out, sem, *, tm, prompt_steps):
    del xs_in

    def scatter_rows(h_ref):
        def row_copy(r, k):
            return pltpu.make_async_copy(h_ref.at[pl.ds(r, 1), :],
                                         xs_out.at[pl.ds(dest_ref[r * TOP_K + k], 1), :], sem)

        def issue(r, c):
            for k in range(TOP_K):
                row_copy(r, k).start()
            return c

        def drain(r, c):
            for k in range(TOP_K):
                row_copy(r, k).wait()
            return c

        lax.fori_loop(0, tm, issue, 0)
        lax.fori_loop(0, tm, drain, 0)

    @pl.when(pl.program_id(0) < prompt_steps)
    def _():
        scatter_rows(hp_ref)

    @pl.when(pl.program_id(0) >= prompt_steps)
    def _():
        scatter_rows(hs_ref)


def _dispatch(h2_p, h2_s, dest_flat, xs_zero):
    tm = TM_TOK
    w = h2_p.shape[1]
    p_steps, s_steps = h2_p.shape[0] // tm, h2_s.shape[0] // tm
    return pl.pallas_call(
        functools.partial(_dispatch_kernel, tm=tm, prompt_steps=p_steps),
        out_shape=jax.ShapeDtypeStruct(xs_zero.shape, U32),
        grid=(p_steps + s_steps,),
        in_specs=[pl.BlockSpec((tm * TOP_K,), lambda i: (i,), memory_space=pltpu.SMEM),
                  pl.BlockSpec((tm, w), lambda i: (jnp.minimum(i, p_steps - 1), 0)),
                  pl.BlockSpec((tm, w), lambda i: (jnp.maximum(i - p_steps, 0), 0)),
                  pl.BlockSpec(memory_space=pl.ANY)],
        out_specs=pl.BlockSpec(memory_space=pl.ANY),
        scratch_shapes=[pltpu.SemaphoreType.DMA(())],
        input_output_aliases={3: 0},
        compiler_params=pltpu.CompilerParams(dimension_semantics=("arbitrary",), has_side_effects=True),
        name="moe_dispatch",
    )(dest_flat, h2_p, h2_s, xs_zero)


def _expert_changed(be_ref, i):
    prev = be_ref[jnp.maximum(i - 1, 0)]
    return jnp.logical_or(i == 0, be_ref[i] != prev)


def _gate_up_kernel(be_ref, nu_ref, xs_ref, wg_ref, wu_ref, bg_ref, bu_ref, act_ref, wg_bf, wu_bf):
    i = pl.program_id(1)

    @pl.when(jnp.logical_and(i < nu_ref[0], _expert_changed(be_ref, i)))
    def _():
        wg_bf[...] = wg_ref[...].astype(BF16)
        wu_bf[...] = wu_ref[...].astype(BF16)

    @pl.when(i < nu_ref[0])
    def _():
        a, b = _unpack_bf16_pairs(xs_ref[...])
        x = jnp.concatenate([a, b], axis=1).astype(BF16)
        gate = jnp.minimum(_dot(x, wg_bf[...]) + bg_ref[...], SWIGLU_LIMIT)
        up = jnp.clip(_dot(x, wu_bf[...]) + bu_ref[...], -SWIGLU_LIMIT, SWIGLU_LIMIT)
        act_ref[...] = ((up + 1.0) * gate * jax.nn.sigmoid(SWIGLU_ALPHA * gate)).astype(act_ref.dtype)

    @pl.when(i >= nu_ref[0])
    def _():
        act_ref[...] = jnp.zeros_like(act_ref)


def _down_kernel(be_ref, nu_ref, act_ref, wa_ref, wb_ref, ba_ref, bb_ref, y_ref, wa_bf, wb_bf):
    i = pl.program_id(1)

    @pl.when(jnp.logical_and(i < nu_ref[0], _expert_changed(be_ref, i)))
    def _():
        wa_bf[...] = wa_ref[...].astype(BF16)
        wb_bf[...] = wb_ref[...].astype(BF16)

    @pl.when(i < nu_ref[0])
    def _():
        act = act_ref[...]
        ya = _dot(act, wa_bf[...]) + ba_ref[...]
        yb = _dot(act, wb_bf[...]) + bb_ref[...]
        y_ref[...] = _pack_bf16_pairs(ya, yb)

    @pl.when(i >= nu_ref[0])
    def _():
        y_ref[...] = jnp.zeros_like(y_ref)


def _experts(xs, block_exp, n_used, w_gu, b_gu, w_down, b_down):
    n_rows, half_d = xs.shape
    d = 2 * half_d
    ne, _, two_f = w_gu.shape
    f = two_f // 2
    tb, tn, tn2 = TB_MOE, TN_GU, TN_DOWN
    n_blk = n_rows // tb
    b_gu3 = b_gu.reshape(ne, 1, two_f)
    b_down3 = b_down.reshape(ne, 1, d)

    def blk(i, nu):
        return jnp.minimum(i, nu[0] - 1)

    up_off = f // tn
    gs1 = pltpu.PrefetchScalarGridSpec(
        num_scalar_prefetch=2, grid=(f // tn, n_blk),
        in_specs=[pl.BlockSpec((tb, half_d), lambda j, i, be, nu: (blk(i, nu), 0)),
                  pl.BlockSpec((None, d, tn), lambda j, i, be, nu: (be[blk(i, nu)], 0, j)),
                  pl.BlockSpec((None, d, tn), lambda j, i, be, nu: (be[blk(i, nu)], 0, j + up_off)),
                  pl.BlockSpec((None, 1, tn), lambda j, i, be, nu: (be[blk(i, nu)], 0, j)),
                  pl.BlockSpec((None, 1, tn), lambda j, i, be, nu: (be[blk(i, nu)], 0, j + up_off))],
        out_specs=pl.BlockSpec((tb, tn), lambda j, i, be, nu: (i, j)),
        scratch_shapes=[pltpu.VMEM((d, tn), BF16), pltpu.VMEM((d, tn), BF16)])
    act = pl.pallas_call(
        _gate_up_kernel, out_shape=jax.ShapeDtypeStruct((n_rows, f), BF16), grid_spec=gs1,
        compiler_params=_cparams(("arbitrary", "arbitrary"), 4 * d * tn * 4 + 2 * d * tn * 2 + 8 * tb * d * 4),
        name="moe_gate_up",
    )(block_exp, n_used, xs, w_gu, w_gu, b_gu3, b_gu3)

    hi_off = half_d // tn2
    gs2 = pltpu.PrefetchScalarGridSpec(
        num_scalar_prefetch=2, grid=(half_d // tn2, n_blk),
        in_specs=[pl.BlockSpec((tb, f), lambda j, i, be, nu: (blk(i, nu), 0)),
                  pl.BlockSpec((None, f, tn2), lambda j, i, be, nu: (be[blk(i, nu)], 0, j)),
                  pl.BlockSpec((None, f, tn2), lambda j, i, be, nu: (be[blk(i, nu)], 0, j + hi_off)),
                  pl.BlockSpec((None, 1, tn2), lambda j, i, be, nu: (be[blk(i, nu)], 0, j)),
                  pl.BlockSpec((None, 1, tn2), lambda j, i, be, nu: (be[blk(i, nu)], 0, j + hi_off))],
        out_specs=pl.BlockSpec((tb, tn2), lambda j, i, be, nu: (i, j)),
        scratch_shapes=[pltpu.VMEM((f, tn2), BF16), pltpu.VMEM((f, tn2), BF16)])
    return pl.pallas_call(
        _down_kernel, out_shape=jax.ShapeDtypeStruct((n_rows, half_d), U32), grid_spec=gs2,
        compiler_params=_cparams(("arbitrary", "arbitrary"), 4 * f * tn2 * 4 + 2 * f * tn2 * 2 + 8 * tb * f * 2),
        name="moe_down",
    )(block_exp, n_used, act, w_down, w_down, b_down3, b_down3)


def _combine_kernel(dest_ref, x1_ref, gt_ref, slab_ref, yb_ref, y_ref, buf, sem, *, tm):
    def row_copy(r, k):
        return pltpu.make_async_copy(yb_ref.at[pl.ds(dest_ref[r * TOP_K + k], 1), :],
                                     buf.at[k, pl.ds(r, 1), :], sem)

    def issue(r, c):
        for k in range(TOP_K):
            row_copy(r, k).start()
        return c

    def drain(r, c):
        for k in range(TOP_K):
            row_copy(r, k).wait()
        return c

    lax.fori_loop(0, tm, issue, 0)
    lax.fori_loop(0, tm, drain, 0)
    slab = slab_ref[...]
    acc_a = None
    for k in range(TOP_K):
        a, b = _unpack_bf16_pairs(buf[k])
        g = slab[:, 2 * TOP_K + k:2 * TOP_K + k + 1]
        acc_a = g * a if acc_a is None else acc_a + g * a
        acc_b = g * b if k == 0 else acc_b + g * b
    moe = jnp.concatenate([acc_a, acc_b], axis=1)
    y_ref[...] = x1_ref[...] + gt_ref[...] * moe


def _combine(x1, gate, slab, dest_flat, yb, *, rows_per_mod, tm):
    n, d = x1.shape
    mod_rows = gate.shape[1]
    return pl.pallas_call(
        functools.partial(_combine_kernel, tm=tm),
        out_shape=jax.ShapeDtypeStruct((n, d), F32),
        grid=(n // tm,),
        in_specs=[pl.BlockSpec((tm * TOP_K,), lambda i: (i,), memory_space=pltpu.SMEM),
                  pl.BlockSpec((tm, d), lambda i: (i, 0)),
                  pl.BlockSpec((None, mod_rows, d), lambda i: (i * tm // rows_per_mod, 0, 0)),
                  pl.BlockSpec((tm, LANES), lambda i: (i, 0)),
                  pl.BlockSpec(memory_space=pl.ANY)],
        out_specs=pl.BlockSpec((tm, d), lambda i: (i, 0)),
        scratch_shapes=[pltpu.VMEM((TOP_K, tm, d // 2), U32), pltpu.SemaphoreType.DMA(())],
        compiler_params=_cparams(("arbitrary",), 8 * tm * d * 4),
        name="moe_combine",
    )(dest_flat, x1, gate, slab, yb)


def _rope_table(pos):
    inv = ROPE_THETA ** (-jnp.arange(0, MLA_ROPE, 2, dtype=F32) / MLA_ROPE)
    ang = pos.astype(F32)[:, None] * inv[None, :]
    c, s = jnp.cos(ang), jnp.sin(ang)
    return jnp.concatenate([c, c, -s, s], axis=-1)


def _swap_halves(a):
    h = a.shape[-1] // 2
    return jnp.concatenate([a[..., h:], a[..., :h]], axis=-1)


def kernel(x_prompt, x_sample, cache_dk, cache_dv, cache_ckv, cache_kpe, page_table, c_prompt, c_sample, norm1_g, norm2_g, w_ada, b_ada, w_in, diff_qn_g, diff_kn_g, diff_lambda, diff_subln_g, mla_qa_g, w_qb, mla_qn_g, mla_kva_g, mla_kpe_g, w_uk, w_uv, w_out, w_router, b_router, w_gu, b_gu, w_down, b_down):
    depth = w_in.shape[0]
    assert depth == 1, "single-layer trunk"
    batch, seq, d = x_prompt.shape
    n_seq, dec_seq, _ = x_sample.shape
    assert dec_seq == 1
    n_pool = cache_ckv.shape[1]
    n_pages = page_table.shape[1]
    past_len = n_pages * PAGE_SIZE
    n_p = batch * seq
    n_all = n_p + n_seq
    lam_init = 0.8 - 0.6 * math.exp(-0.3 * 0)
    assert seq % TM_PROJ == 0 and seq % TQ_ATTN == 0 and n_p % TM_TOK == 0 and n_seq % TM_TOK == 0

    w_in0 = w_in[0]
    w_in_ext = jnp.concatenate([w_in0, _swap_halves(w_in0[:, O_KPE:])], axis=1).astype(BF16)
    gmat = (jnp.arange(LANES)[:, None] // DIFF_DH == jnp.arange(LANES)[None, :] // DIFF_DH).astype(BF16)
    gdq = (jnp.tile(diff_qn_g[0].reshape(-1), DIFF_HEADS) * (DIFF_DH ** -0.5)).reshape(1, -1)
    gdk = jnp.tile(diff_kn_g[0].reshape(-1), DIFF_KV_HEADS).reshape(1, -1)
    gkpe = jnp.concatenate([mla_kpe_g[0], _swap_halves(mla_kpe_g[0])]).reshape(1, -1)
    qscale = MLA_QK_DIM ** -0.5
    gqn = (mla_qn_g[0, :MLA_NOPE] * qscale).reshape(1, -1)
    gq_pe = mla_qn_g[0, MLA_NOPE:]
    gqp = (jnp.concatenate([gq_pe, _swap_halves(gq_pe)]) * qscale).reshape(1, -1)
    wqb3 = w_qb[0].reshape(Q_LORA, MLA_HEADS, MLA_QK_DIM)
    wqb_pe = wqb3[:, :, MLA_NOPE:]
    wqb_ext = jnp.concatenate([wqb3[:, :, :MLA_NOPE].reshape(Q_LORA, -1),
                               jnp.concatenate([wqb_pe, _swap_halves(wqb_pe)], axis=-1).reshape(Q_LORA, -1)],
                              axis=1).astype(BF16)
    w_kvup = jnp.concatenate([jnp.transpose(w_uk[0], (1, 0, 2)).reshape(KV_LORA, -1),
                              jnp.transpose(w_uv[0], (1, 0, 2)).reshape(KV_LORA, -1)], axis=1).astype(BF16)
    w_uk_t = jnp.transpose(w_uk[0], (0, 2, 1)).astype(BF16)
    w_uv_bf = w_uv[0].astype(BF16)
    proj_consts = [norm1_g[0].reshape(1, -1), w_in_ext, gmat, gdq, gdk, mla_qa_g[0].reshape(1, -1),
                   mla_kva_g[0].reshape(1, -1), gkpe, wqb_ext, gqn, gqp]
    lam_v = diff_lambda[0]
    gsub = diff_subln_g[0].reshape(1, -1)

    mod = _modulation(jnp.concatenate([c_prompt, c_sample], axis=0), w_ada[0], b_ada[0])
    mod_p = mod[:batch].reshape(batch, 6, 1, d)
    mod_s = mod[batch:].reshape(n_seq, 6, d)
    sh1_p, sc1_p, gt1_p, sh2_p, sc2_p, gt2_p = [mod_p[:, t] for t in range(6)]
    sh1_s, sc1_s, gt1_s, sh2_s, sc2_s, gt2_s = [mod_s[:, t][None] for t in range(6)]

    xp2 = x_prompt.reshape(n_p, d)
    xs2 = x_sample.reshape(n_seq, d)
    tab_p = _rope_table(jnp.arange(seq, dtype=I32))
    tab_s = _rope_table(jnp.full((1,), past_len, I32))
    (qds, dk_p, dv_p, ckv_p, kpe_p, qmla, kmla, vmla) = _project(
        xp2, sh1_p, sc1_p, tab_p, proj_consts, w_kvup, sample=False, rows_per_mod=seq, tm=TM_PROJ)
    (qd_s, dk_s, dv_s, ckv_s, kpe_s, qlat_s, qpe_s) = _project(
        xs2, sh1_s, sc1_s, tab_s, proj_consts, w_uk_t, sample=True, rows_per_mod=n_seq, tm=n_seq)

    od_p = _diff_prompt_attention(qds, dk_p, dv_p, lam_v, gsub, batch=batch, seq=seq, lam_init=lam_init)
    om_p = _mla_prompt_attention(qmla, kmla, vmla, batch=batch, seq=seq)

    c_ckv = cache_ckv.reshape(n_pool, PAGE_SIZE, KV_LORA)
    c_kpe = jnp.transpose(cache_kpe, (0, 1, 3, 2)).reshape(n_pool, MLA_ROPE, PAGE_SIZE)
    c_dk = jnp.transpose(cache_dk, (0, 1, 3, 4, 5, 2)).reshape(n_pool, DIFF_K_COLS, PAGE_SIZE)
    c_dv = cache_dv.reshape(n_pool, PAGE_SIZE * DIFF_KV_HEADS, HEAD_DIM)
    q5 = qd_s.reshape(n_seq, DIFF_KV_HEADS, DIFF_GROUP, 2, DIFF_DH)
    eye_kv = jnp.eye(DIFF_KV_HEADS, dtype=F32)
    eye_m = jnp.eye(2, dtype=F32)
    qdm = jnp.einsum("skgmd,kK,mM->skmgKMd", q5, eye_kv, eye_m).reshape(n_seq, 2 * DIFF_HEADS, DIFF_K_COLS)
    head_id = jnp.arange(2 * DIFF_HEADS) // (2 * DIFF_GROUP) * DIFF_GROUP + jnp.arange(2 * DIFF_HEADS) % DIFF_GROUP
    slopes = jnp.broadcast_to((2.0 ** (-(head_id + 1).astype(F32)))[:, None], (2 * DIFF_HEADS, LANES))
    qpe3 = qpe_s.reshape(n_seq, MLA_HEADS, LANES)[:, :, :MLA_ROPE]
    olat_s, od_s = _decode_attention(
        page_table, lam_v, gsub, slopes, qlat_s.reshape(n_seq, MLA_HEADS, KV_LORA), qpe3,
        ckv_s.reshape(n_seq, 1, KV_LORA), kpe_s.reshape(n_seq, 1, MLA_ROPE), qdm,
        dk_s.reshape(n_seq, 1, DIFF_K_COLS), dv_s.reshape(n_seq, 1, DIFF_V_COLS),
        c_ckv, c_kpe, c_dk, c_dv, lam_init=lam_init)
    om_s = _latent_out(olat_s.reshape(n_seq, MLA_HEADS * KV_LORA), w_uv_bf)
    od_s = od_s.reshape(n_seq, DIFF_HEADS * HEAD_DIM).astype(BF16)

    tri = (jnp.arange(TM_PROJ)[:, None] > jnp.arange(TM_PROJ)[None, :]).astype(BF16)
    out_consts = [norm2_g[0].reshape(1, -1), w_out[0].astype(BF16), w_router[0], b_router[0].reshape(1, -1)]
    zero_counts = jnp.zeros((1, N_EXPERTS), F32)
    x1_p, h2_p, slab_p, counts_p = _out_project(
        od_p, om_p, xp2, gt1_p, sh2_p, sc2_p, out_consts + [tri], zero_counts,
        rows_per_mod=seq, tm=TM_PROJ, name="out_proj_router_prompt")
    x1_s, h2_s, slab_s, counts = _out_project(
        od_s, om_s, xs2, gt1_s, sh2_s, sc2_s, out_consts + [tri[:n_seq, :n_seq]], counts_p,
        rows_per_mod=n_seq, tm=n_seq, name="out_proj_router_sample")

    tb = TB_MOE
    n_blk = (n_all * TOP_K + N_EXPERTS * (tb - 1)) // tb + 1
    cnt = counts[0].astype(I32)
    padded = (cnt + tb - 1) // tb * tb
    pad_end = jnp.cumsum(padded)
    pad_start = pad_end - padded
    slab = jnp.concatenate([slab_p, slab_s], axis=0)
    e_idx = slab[:, :TOP_K].astype(I32)
    rank = slab[:, TOP_K:2 * TOP_K].astype(I32)
    dest = (pad_start[e_idx] + rank).reshape(-1)
    n_used = (pad_end[-1] // tb).astype(I32).reshape(1)
    block_exp = jnp.minimum(jnp.searchsorted(pad_end, jnp.arange(n_blk, dtype=I32) * tb, side="right"),
                            N_EXPERTS - 1).astype(I32)

    xs = _dispatch(h2_p, h2_s, dest, jnp.zeros((n_blk * tb, d // 2), U32))
    yb = _experts(xs, block_exp, n_used, w_gu[0], b_gu[0], w_down[0], b_down[0])
    y_p = _combine(x1_p, gt2_p, slab_p, dest[:n_p * TOP_K], yb, rows_per_mod=seq, tm=TM_TOK)
    y_s = _combine(x1_s, gt2_s, slab_s, dest[n_p * TOP_K:], yb, rows_per_mod=n_seq, tm=TM_TOK)

    def kv_outputs(dk, dv, ckv, kpe, b, t):
        return (dk.reshape(1, b, t, DIFF_KV_HEADS, 2, DIFF_DH), dv.reshape(1, b, t, DIFF_KV_HEADS, HEAD_DIM),
                ckv.reshape(1, b, t, KV_LORA), kpe.reshape(1, b, t, MLA_ROPE))

    return ((y_p.reshape(batch, seq, d), y_s.reshape(n_seq, 1, d))
            + kv_outputs(dk_p, dv_p, ckv_p, kpe_p, batch, seq) + kv_outputs(dk_s, dv_s, ckv_s, kpe_s, n_seq, 1))
```

```python
import functools
import math

import jax
import jax.numpy as jnp
from jax import lax
from jax.experimental import pallas as pl
from jax.experimental.pallas import tpu as pltpu

F32 = jnp.float32
BF16 = jnp.bfloat16
U32 = jnp.uint32
I32 = jnp.int32

LANES = 128
V7X_VMEM_BYTES = 64 * 1024 * 1024

HEAD_DIM = 128
DIFF_KV_HEADS = 2
DIFF_GROUP = 4
DIFF_HEADS = DIFF_KV_HEADS * DIFF_GROUP
DIFF_DH = 64
MLA_HEADS = 8
Q_LORA = 512
KV_LORA = 512
MLA_NOPE = 128
MLA_ROPE = 64
MLA_QK_DIM = MLA_NOPE + MLA_ROPE
ROPE_THETA = 10000.0
N_EXPERTS = 32
TOP_K = 4
SWIGLU_LIMIT = 7.0
SWIGLU_ALPHA = 1.702
PAGE_SIZE = 128
RMS_EPS = 1e-6
NEG_INF = -1e30

DIFF_Q_COLS = DIFF_HEADS * 2 * DIFF_DH
DIFF_K_COLS = DIFF_KV_HEADS * 2 * DIFF_DH
DIFF_V_COLS = DIFF_KV_HEADS * HEAD_DIM
O_DQ = 0
O_DK = O_DQ + DIFF_Q_COLS
O_DV = O_DK + DIFF_K_COLS
O_MQ = O_DV + DIFF_V_COLS
O_CKV = O_MQ + Q_LORA
O_KPE = O_CKV + KV_LORA
IN_COLS = O_KPE + MLA_ROPE
IN_COLS_EXT = IN_COLS + MLA_ROPE

TM_PROJ = 256
TQ_DIFF = 256
TQ_MLA = 256
CHAIN_MLA = 128
TK_ATTN = 256
TM_TOK = 128
DMA_UNROLL = 8
TB_MOE = 256
TN_GU = 1024
TN_DOWN = 1024
TN_ADA = 1024
PAGES_PER_CHUNK = 16


def _vmem_limit(nbytes):
    return int(min(max(nbytes * 5 // 4, 16 * 1024 * 1024), V7X_VMEM_BYTES - 8 * 1024 * 1024))


def _cparams(sem, vmem_bytes):
    return pltpu.CompilerParams(dimension_semantics=sem, vmem_limit_bytes=_vmem_limit(vmem_bytes))


def _dot(a, b):
    return jnp.dot(a, b, preferred_element_type=F32)


def _dot_nt(a, b):
    return lax.dot_general(a, b, (((1,), (1,)), ((), ())), preferred_element_type=F32)


def _lane_iota(shape):
    return lax.broadcasted_iota(I32, shape, len(shape) - 1)


def _mod_kernel(c_ref, w_ref, b_ref, o_ref):
    c = c_ref[...]
    a = (c * jax.nn.sigmoid(c)).astype(BF16)
    o_ref[...] = _dot(a, w_ref[...].astype(BF16)) + b_ref[...]


def _modulation(c_all, w_ada, b_ada):
    r, d = c_all.shape
    n = w_ada.shape[1]
    tn = TN_ADA
    return pl.pallas_call(
        _mod_kernel,
        out_shape=jax.ShapeDtypeStruct((r, n), F32),
        grid=(n // tn,),
        in_specs=[pl.BlockSpec((r, d), lambda j: (0, 0)),
                  pl.BlockSpec((d, tn), lambda j: (0, j)),
                  pl.BlockSpec((1, tn), lambda j: (0, j))],
        out_specs=pl.BlockSpec((r, tn), lambda j: (0, j)),
        compiler_params=_cparams(("arbitrary",), 2 * d * tn * 4 + d * tn * 2 + 4 * r * d * 4),
        name="adaln_modulation",
    )(c_all, w_ada, b_ada.reshape(1, n))


def _group64_sumsq(x, gmat):
    x2 = (x * x).astype(BF16)
    outs = [_dot(x2[:, j * LANES:(j + 1) * LANES], gmat) for j in range(x.shape[1] // LANES)]
    return outs[0] if len(outs) == 1 else jnp.concatenate(outs, axis=1)


def _rope_pair_block(blk, tab):
    r = blk * tab
    return r + pltpu.roll(r, 64, 1)


def _proj_kernel(x_ref, sh_ref, sc_ref, tab_ref, g1_ref, win_ref, gmat_ref, gdq_ref, gdk_ref, gqa_ref,
                 gckv_ref, gkpe_ref, wqb_ref, gqn_ref, gqp_ref, wup_ref, *out_refs, sample):
    x = x_ref[...]
    h = x * lax.rsqrt(jnp.mean(x * x, axis=-1, keepdims=True) + RMS_EPS) * g1_ref[...]
    h = h * (1.0 + sc_ref[...]) + sh_ref[...]
    z = _dot(h.astype(BF16), win_ref[...])
    gmat = gmat_ref[...]
    tab = tab_ref[...]
    lane = _lane_iota((1, LANES))
    low_half = lane < 64

    dq = z[:, O_DQ:O_DK]
    qd = dq * lax.rsqrt(_group64_sumsq(dq, gmat) * (1.0 / DIFF_DH) + RMS_EPS) * gdq_ref[...]
    dk = z[:, O_DK:O_DV]
    kd = dk * lax.rsqrt(_group64_sumsq(dk, gmat) * (1.0 / DIFF_DH) + RMS_EPS) * gdk_ref[...]
    vd = z[:, O_DV:O_MQ]
    mq = z[:, O_MQ:O_CKV]
    ckv = z[:, O_CKV:O_KPE]
    kpb = z[:, O_KPE:IN_COLS_EXT]

    ckv_n = ckv * lax.rsqrt(jnp.mean(ckv * ckv, axis=-1, keepdims=True) + RMS_EPS) * gckv_ref[...]
    kp_ms = 0.5 * jnp.sum(kpb * kpb, axis=-1, keepdims=True) * (1.0 / MLA_ROPE)
    kpn = kpb * lax.rsqrt(kp_ms + RMS_EPS) * gkpe_ref[...]
    kr = _rope_pair_block(kpn, tab)

    mqn = (mq * lax.rsqrt(jnp.mean(mq * mq, axis=-1, keepdims=True) + RMS_EPS) * gqa_ref[...]).astype(BF16)
    qm = _dot(mqn, wqb_ref[...])
    gqn = gqn_ref[...]
    gqp = gqp_ref[...]
    q_nope, q_rope = [], []
    for hh in range(MLA_HEADS):
        n_h = qm[:, hh * LANES:(hh + 1) * LANES]
        p_h = qm[:, (MLA_HEADS + hh) * LANES:(MLA_HEADS + hh + 1) * LANES]
        ms = (jnp.sum(n_h * n_h, axis=-1, keepdims=True)
              + 0.5 * jnp.sum(p_h * p_h, axis=-1, keepdims=True)) * (1.0 / MLA_QK_DIM)
        r = lax.rsqrt(ms + RMS_EPS)
        q_nope.append(n_h * r * gqn)
        q_rope.append(_rope_pair_block(p_h * r * gqp, tab))

    if sample:
        qd_ref, dk_ref, dv_ref, ckv_ref, kpe_ref, qlat_ref, qpe_ref = out_refs
        qd_ref[...] = qd
        qlat_ref[...] = jnp.concatenate(
            [_dot(q_nope[hh].astype(BF16), wup_ref[hh]) for hh in range(MLA_HEADS)], axis=1)
        qpe_ref[...] = jnp.concatenate(q_rope, axis=1)
    else:
        qds_ref, dk_ref, dv_ref, ckv_ref, kpe_ref, qmla_ref, kmla_ref, vmla_ref = out_refs
        blocks = []
        for c in range(DIFF_HEADS):
            blk = qd[:, c * LANES:(c + 1) * LANES]
            blocks.append(jnp.where(low_half, blk, 0.0))
            blocks.append(jnp.where(low_half, 0.0, blk))
        qds_ref[...] = jnp.concatenate(blocks, axis=1).astype(BF16)
        kvup = _dot(ckv_n.astype(BF16), wup_ref[...])
        kr_lo = jnp.where(low_half, kr, 0.0)
        qb, kb = [], []
        for hh in range(MLA_HEADS):
            qb += [q_nope[hh], q_rope[hh]]
            kb += [kvup[:, hh * LANES:(hh + 1) * LANES], kr_lo]
        qmla_ref[...] = jnp.concatenate(qb, axis=1).astype(BF16)
        kmla_ref[...] = jnp.concatenate(kb, axis=1).astype(BF16)
        vmla_ref[...] = kvup[:, MLA_HEADS * LANES:].astype(BF16)
    dk_ref[...] = kd
    dv_ref[...] = vd
    ckv_ref[...] = ckv_n
    kpe_ref[...] = kr[:, :MLA_ROPE]


def _project(x2d, shift, scale, tab, consts, wup, *, sample, rows_per_mod, tm):
    n, d = x2d.shape
    steps = n // tm
    mod_rows = shift.shape[1]
    tab_rows = tab.shape[0]
    tab_block = tm if tab_rows > 1 else 1
    tab_steps = max(tab_rows // tm, 1)
    mod_spec = pl.BlockSpec((None, mod_rows, d), lambda i: (i * tm // rows_per_mod, 0, 0))
    tab_spec = pl.BlockSpec((tab_block, LANES), lambda i: (i % tab_steps, 0))

    def full(a):
        nd = a.ndim
        return pl.BlockSpec(a.shape, lambda i: (0,) * nd)

    def rows(cols, dt):
        return jax.ShapeDtypeStruct((n, cols), dt), pl.BlockSpec((tm, cols), lambda i: (i, 0))

    if sample:
        outs = [rows(DIFF_Q_COLS, F32), rows(DIFF_K_COLS, F32), rows(DIFF_V_COLS, F32), rows(KV_LORA, F32),
                rows(MLA_ROPE, F32), rows(MLA_HEADS * KV_LORA, F32), rows(MLA_HEADS * LANES, F32)]
    else:
        outs = [rows(2 * DIFF_Q_COLS, BF16), rows(DIFF_K_COLS, F32), rows(DIFF_V_COLS, F32), rows(KV_LORA, F32),
                rows(MLA_ROPE, F32), rows(2 * MLA_HEADS * LANES, BF16), rows(2 * MLA_HEADS * LANES, BF16),
                rows(MLA_HEADS * LANES, BF16)]
    out_shape = [o[0] for o in outs]
    out_specs = [o[1] for o in outs]
    in_specs = [pl.BlockSpec((tm, d), lambda i: (i, 0)), mod_spec, mod_spec, tab_spec] + [full(a) for a in consts] + [full(wup)]
    weight_bytes = sum(int(a.size) * a.dtype.itemsize for a in consts) + int(wup.size) * wup.dtype.itemsize
    vmem = 2 * weight_bytes + 2 * tm * d * 4 + 12 * tm * IN_COLS_EXT * 4
    return pl.pallas_call(
        functools.partial(_proj_kernel, sample=sample),
        out_shape=out_shape, grid=(steps,), in_specs=in_specs, out_specs=out_specs,
        compiler_params=_cparams(("arbitrary",), vmem),
        name="in_proj_sample" if sample else "in_proj_prompt",
    )(x2d, shift, scale, tab, *consts, wup)


def _softmax_step(s, v_bf, m, l, acc):
    m_new = jnp.maximum(m, jnp.max(s, axis=-1, keepdims=True))
    alpha = jnp.exp(m - m_new)
    p = jnp.exp(s - m_new)
    l_new = alpha * l + jnp.sum(p, axis=-1, keepdims=True)
    acc_new = alpha * acc + _dot(p.astype(BF16), v_bf)
    return m_new, l_new, acc_new


def _diff_lambda(lv, lam_init):
    a = jnp.sum(lv[0:1, :] * lv[1:2, :], axis=-1, keepdims=True)
    b = jnp.sum(lv[2:3, :] * lv[3:4, :], axis=-1, keepdims=True)
    return jnp.exp(a) - jnp.exp(b) + lam_init


def _causal_blocks(i, tq, tk):
    assert tq % tk == 0 or tk % tq == 0
    return (i * tq) // tk, max(tq // tk, 1)


def _chain_stats(s, m_old, l_old):
    m_new = jnp.maximum(m_old, jnp.max(s, axis=-1, keepdims=True))
    alpha = jnp.exp(m_old - m_new)
    p = jnp.exp(s - m_new)
    return m_new, alpha * l_old + jnp.sum(p, axis=-1, keepdims=True), p.astype(BF16), alpha


def _chain_init(n_chains, rows):
    return tuple((jnp.full((rows, 1), NEG_INF, F32), jnp.zeros((rows, 1), F32), jnp.zeros((rows, HEAD_DIM), F32))
                 for _ in range(n_chains))


def _diff_prompt_kernel(lam_ref, gsub_ref, q_ref, k_ref, v_ref, o_ref, *, tq, tk, lam_init):
    kvh = pl.program_id(1)
    i = pl.program_id(2)
    q = q_ref[...]
    nrow = 2 * DIFF_GROUP
    qs = jnp.concatenate([q[:, c * LANES:(c + 1) * LANES] for c in range(nrow)], axis=0)
    kv_scale = jnp.where(kvh == 0, 1.0, 2.0 ** (-DIFF_GROUP)).astype(F32)
    rel = (lax.broadcasted_iota(I32, (tq, tk), 0) - lax.broadcasted_iota(I32, (tq, tk), 1))

    def step(j, carry, masked):
        m, l, acc = carry
        start = pl.multiple_of(j * tk, tk)
        k = k_ref[pl.ds(start, tk), :].astype(BF16)
        v = v_ref[pl.ds(start, tk), :].astype(BF16)
        s = _dot_nt(qs, k)
        dist = rel + (i * tq - j * tk)
        distf = dist.astype(F32)
        pieces = []
        for g in range(DIFF_GROUP):
            bias = (2.0 ** (-(g + 1))) * kv_scale * distf
            for mp in range(2):
                r0 = (g * 2 + mp) * tq
                sg = s[r0:r0 + tq] - bias
                if masked:
                    sg = jnp.where(dist >= 0, sg, NEG_INF)
                pieces.append(sg)
        s = jnp.concatenate(pieces, axis=0)
        return _softmax_step(s, v, m, l, acc)

    init = (jnp.full((nrow * tq, 1), NEG_INF, F32), jnp.zeros((nrow * tq, 1), F32),
            jnp.zeros((nrow * tq, HEAD_DIM), F32))
    n_full, n_diag = _causal_blocks(i, tq, tk)
    carry = lax.fori_loop(0, n_full, lambda j, c: step(j, c, False), init)
    for jj in range(n_diag):
        carry = step(n_full + jj, carry, True)
    m, l, acc = carry
    o = acc / l
    lam = _diff_lambda(lam_ref[...], lam_init)
    gsub = gsub_ref[...] * (1.0 - lam_init)
    outs = []
    for g in range(DIFF_GROUP):
        og = o[(2 * g) * tq:(2 * g + 1) * tq] - lam * o[(2 * g + 1) * tq:(2 * g + 2) * tq]
        og = og * lax.rsqrt(jnp.mean(og * og, axis=-1, keepdims=True) + RMS_EPS) * gsub
        outs.append(og)
    o_ref[...] = jnp.concatenate(outs, axis=1).astype(o_ref.dtype)


def _diff_prompt_attention(qds, dk, dv, lam_v, gsub, *, batch, seq, lam_init):
    tq, tk = TQ_DIFF, TK_ATTN
    nq = seq // tq
    nrow = 2 * DIFF_GROUP
    width = nrow * LANES
    vmem = 2 * (tq * width * 2 + 2 * seq * LANES * 4) + 6 * nrow * tq * tk * 4 + 3 * nrow * tq * LANES * 4
    return pl.pallas_call(
        functools.partial(_diff_prompt_kernel, tq=tq, tk=tk, lam_init=lam_init),
        out_shape=jax.ShapeDtypeStruct((batch * seq, DIFF_HEADS * HEAD_DIM), BF16),
        grid=(batch, DIFF_KV_HEADS, nq),
        in_specs=[pl.BlockSpec(lam_v.shape, lambda b, h, i: (0, 0)),
                  pl.BlockSpec(gsub.shape, lambda b, h, i: (0, 0)),
                  pl.BlockSpec((tq, width), lambda b, h, i: (b * nq + i, h)),
                  pl.BlockSpec((seq, LANES), lambda b, h, i: (b, h)),
                  pl.BlockSpec((seq, HEAD_DIM), lambda b, h, i: (b, h))],
        out_specs=pl.BlockSpec((tq, DIFF_GROUP * HEAD_DIM), lambda b, h, i: (b * nq + i, h)),
        compiler_params=_cparams(("arbitrary", "arbitrary", "arbitrary"), vmem),
        name="diff_attn_prompt",
    )(lam_v, gsub, qds, dk, dv)


def _mla_prompt_kernel(q_ref, k_ref, v_ref, o_ref, *, tq, tk, chain):
    i = pl.program_id(1)
    qk = 2 * LANES
    per_head = tq // chain
    rel = (lax.broadcasted_iota(I32, (chain, tk), 0) - lax.broadcasted_iota(I32, (chain, tk), 1))

    def step(j, state, masked):
        start = pl.multiple_of(j * tk, tk)
        scores = []
        for hh in range(MLA_HEADS):
            k = k_ref[pl.ds(start, tk), hh * qk:(hh + 1) * qk]
            for r in range(per_head):
                scores.append(_dot_nt(q_ref[r * chain:(r + 1) * chain, hh * qk:(hh + 1) * qk], k))
        stats = []
        for hh in range(MLA_HEADS):
            for r in range(per_head):
                c = hh * per_head + r
                s = scores[c]
                if masked:
                    s = jnp.where(rel + (i * tq + r * chain - j * tk) >= 0, s, NEG_INF)
                stats.append(_chain_stats(s, state[c][0], state[c][1]))
        new = []
        for hh in range(MLA_HEADS):
            v = v_ref[pl.ds(start, tk), hh * HEAD_DIM:(hh + 1) * HEAD_DIM]
            for r in range(per_head):
                c = hh * per_head + r
                m_new, l_new, p, alpha = stats[c]
                new.append((m_new, l_new, alpha * state[c][2] + _dot(p, v)))
        return tuple(new)

    n_full, n_diag = _causal_blocks(i, tq, tk)
    state = lax.fori_loop(0, n_full, lambda j, st: step(j, st, False), _chain_init(MLA_HEADS * per_head, chain))
    for jj in range(n_diag):
        state = step(n_full + jj, state, True)
    for hh in range(MLA_HEADS):
        o_h = jnp.concatenate([state[hh * per_head + r][2] / state[hh * per_head + r][1] for r in range(per_head)],
                              axis=0)
        o_ref[:, hh * HEAD_DIM:(hh + 1) * HEAD_DIM] = o_h.astype(o_ref.dtype)


def _mla_prompt_attention(qmla, kmla, vmla, *, batch, seq):
    tq, tk, chain = TQ_MLA, TK_ATTN, CHAIN_MLA
    nq = seq // tq
    qk = 2 * LANES
    vmem = (2 * (tq * MLA_HEADS * qk * 2 + seq * MLA_HEADS * qk * 2 + seq * MLA_HEADS * HEAD_DIM * 2
                 + tq * MLA_HEADS * HEAD_DIM * 2) + 3 * MLA_HEADS * tq * LANES * 4 + 16 * chain * tk * 4)
    return pl.pallas_call(
        functools.partial(_mla_prompt_kernel, tq=tq, tk=tk, chain=chain),
        out_shape=jax.ShapeDtypeStruct((batch * seq, MLA_HEADS * HEAD_DIM), BF16),
        grid=(batch, nq),
        in_specs=[pl.BlockSpec((tq, MLA_HEADS * qk), lambda b, i: (b * nq + i, 0)),
                  pl.BlockSpec((seq, MLA_HEADS * qk), lambda b, i: (b, 0)),
                  pl.BlockSpec((seq, MLA_HEADS * HEAD_DIM), lambda b, i: (b, 0))],
        out_specs=pl.BlockSpec((tq, MLA_HEADS * HEAD_DIM), lambda b, i: (b * nq + i, 0)),
        compiler_params=_cparams(("arbitrary", "arbitrary"), vmem),
        name="mla_attn_prompt",
    )(qmla, kmla, vmla)


def _decode_kernel(pt_ref, lam_ref, gsub_ref, slope_ref, qlat_ref, qpe_ref, ckvn_ref, kpen_ref, qdm_ref,
                   kdn_ref, vdn_ref, c_ckv, c_kpe, c_dk, c_dv, olat_ref, od_ref,
                   ckv_buf, kpe_buf, dk_buf, dv_buf, sem, *, n_seq, n_chunks, pages, lam_init):
    s_idx = pl.program_id(0)
    chunk_len = pages * PAGE_SIZE

    def copies(seq, chunk, slot):
        out = []
        for p in range(pages):
            page = pt_ref[seq, chunk * pages + p]
            out.append(pltpu.make_async_copy(
                c_ckv.at[page], ckv_buf.at[slot, pl.ds(p * PAGE_SIZE, PAGE_SIZE), :], sem.at[slot]))
            out.append(pltpu.make_async_copy(
                c_kpe.at[page], kpe_buf.at[slot, :, pl.ds(p * PAGE_SIZE, PAGE_SIZE)], sem.at[slot]))
            out.append(pltpu.make_async_copy(
                c_dk.at[page], dk_buf.at[slot, :, pl.ds(p * PAGE_SIZE, PAGE_SIZE)], sem.at[slot]))
            out.append(pltpu.make_async_copy(
                c_dv.at[page], dv_buf.at[slot, pl.ds(2 * p * PAGE_SIZE, 2 * PAGE_SIZE), :], sem.at[slot]))
        return out

    def start_chunk(seq, chunk, slot):
        for cp in copies(seq, chunk, slot):
            cp.start()

    def wait_chunk(seq, chunk, slot):
        for cp in copies(seq, chunk, slot):
            cp.wait()

    @pl.when(s_idx == 0)
    def _():
        start_chunk(0, 0, 0)

    qlat = qlat_ref[...]
    qpe = qpe_ref[...]
    qdm = qdm_ref[...]
    qlat_bf, qpe_bf, qdm_bf = qlat.astype(BF16), qpe.astype(BF16), qdm.astype(BF16)
    ckv_new = ckvn_ref[...]
    slopes = slope_ref[...][:, 0:1]
    past_len = n_chunks * chunk_len

    m_m = (jnp.sum(qlat * ckv_new, axis=-1, keepdims=True)
           + jnp.sum(qpe * kpen_ref[...], axis=-1, keepdims=True))
    l_m = jnp.ones_like(m_m)
    acc_m = jnp.broadcast_to(ckv_new, (MLA_HEADS, KV_LORA)).astype(F32)
    m_d = jnp.sum(qdm * kdn_ref[...], axis=-1, keepdims=True)
    l_d = jnp.ones_like(m_d)
    vdn = vdn_ref[...]
    half = 2 * DIFF_GROUP
    acc_d = jnp.concatenate([jnp.broadcast_to(vdn[:, :HEAD_DIM], (half, HEAD_DIM)),
                             jnp.broadcast_to(vdn[:, HEAD_DIM:], (half, HEAD_DIM))], axis=0).astype(F32)

    def body(c, carry):
        m_m, l_m, acc_m, m_d, l_d, acc_d = carry
        step = s_idx * n_chunks + c
        slot = step % 2

        @pl.when(c + 1 < n_chunks)
        def _():
            start_chunk(s_idx, c + 1, 1 - slot)

        @pl.when(jnp.logical_and(c + 1 == n_chunks, s_idx + 1 < n_seq))
        def _():
            start_chunk(s_idx + 1, 0, 1 - slot)

        wait_chunk(s_idx, c, slot)

        ckv = ckv_buf[slot].astype(BF16)
        s_m = _dot_nt(qlat_bf, ckv) + _dot(qpe_bf, kpe_buf[slot].astype(BF16))
        m_m, l_m, acc_m = _softmax_step(s_m, ckv, m_m, l_m, acc_m)

        s_d = _dot(qdm_bf, dk_buf[slot].astype(BF16))
        pos = c * chunk_len + _lane_iota((1, chunk_len))
        s_d = s_d - slopes * (past_len - pos).astype(F32)
        m_new = jnp.maximum(m_d, jnp.max(s_d, axis=-1, keepdims=True))
        alpha = jnp.exp(m_d - m_new)
        p = jnp.exp(s_d - m_new)
        l_d = alpha * l_d + jnp.sum(p, axis=-1, keepdims=True)
        p_bf = p.astype(BF16)
        v0 = dv_buf[slot, pl.ds(0, chunk_len, stride=2), :].astype(BF16)
        v1 = dv_buf[slot, pl.ds(1, chunk_len, stride=2), :].astype(BF16)
        pv = jnp.concatenate([_dot(p_bf[:half], v0), _dot(p_bf[half:], v1)], axis=0)
        acc_d = alpha * acc_d + pv
        return m_m, l_m, acc_m, m_new, l_d, acc_d

    m_m, l_m, acc_m, m_d, l_d, acc_d = lax.fori_loop(0, n_chunks, body, (m_m, l_m, acc_m, m_d, l_d, acc_d))

    olat_ref[...] = acc_m / l_m
    o = acc_d / l_d
    lam = _diff_lambda(lam_ref[...], lam_init)
    g4 = DIFF_GROUP
    od = jnp.concatenate([o[0:g4] - lam * o[g4:2 * g4], o[2 * g4:3 * g4] - lam * o[3 * g4:4 * g4]], axis=0)
    od = od * lax.rsqrt(jnp.mean(od * od, axis=-1, keepdims=True) + RMS_EPS) * (gsub_ref[...] * (1.0 - lam_init))
    od_ref[...] = od


def _decode_attention(page_table, lam_v, gsub, slopes, qlat, qpe, ckv_new, kpe_new, qdm, kd_new, vd_new,
                      c_ckv, c_kpe, c_dk, c_dv, *, lam_init):
    n_seq, n_pages = page_table.shape
    pages = min(PAGES_PER_CHUNK, n_pages)
    n_chunks = n_pages // pages
    chunk_len = pages * PAGE_SIZE

    def per_seq(a):
        return pl.BlockSpec((None,) + a.shape[1:], lambda s, pt: (s,) + (0,) * (a.ndim - 1))

    def full(a):
        return pl.BlockSpec(a.shape, lambda s, pt: (0,) * a.ndim)

    any_spec = pl.BlockSpec(memory_space=pl.ANY)
    buf_bytes = 2 * chunk_len * (KV_LORA + MLA_ROPE + DIFF_K_COLS + DIFF_V_COLS) * 4
    grid_spec = pltpu.PrefetchScalarGridSpec(
        num_scalar_prefetch=1,
        grid=(n_seq,),
        in_specs=[full(lam_v), full(gsub), full(slopes), per_seq(qlat), per_seq(qpe), per_seq(ckv_new),
                  per_seq(kpe_new), per_seq(qdm), per_seq(kd_new), per_seq(vd_new),
                  any_spec, any_spec, any_spec, any_spec],
        out_specs=[pl.BlockSpec((None, MLA_HEADS, KV_LORA), lambda s, pt: (s, 0, 0)),
                   pl.BlockSpec((None, DIFF_HEADS, HEAD_DIM), lambda s, pt: (s, 0, 0))],
        scratch_shapes=[pltpu.VMEM((2, chunk_len, KV_LORA), F32),
                        pltpu.VMEM((2, MLA_ROPE, chunk_len), F32),
                        pltpu.VMEM((2, DIFF_K_COLS, chunk_len), F32),
                        pltpu.VMEM((2, 2 * chunk_len, HEAD_DIM), F32),
                        pltpu.SemaphoreType.DMA((2,))],
    )
    return pl.pallas_call(
        functools.partial(_decode_kernel, n_seq=n_seq, n_chunks=n_chunks, pages=pages, lam_init=lam_init),
        out_shape=[jax.ShapeDtypeStruct((n_seq, MLA_HEADS, KV_LORA), F32),
                   jax.ShapeDtypeStruct((n_seq, DIFF_HEADS, HEAD_DIM), F32)],
        grid_spec=grid_spec,
        compiler_params=_cparams(("arbitrary",), buf_bytes + buf_bytes // 2),
        name="decode_attn",
    )(page_table, lam_v, gsub, slopes, qlat, qpe, ckv_new, kpe_new, qdm, kd_new, vd_new, c_ckv, c_kpe, c_dk, c_dv)


def _latent_out_kernel(o_ref, w_ref, out_ref):
    out_ref[...] = _dot(o_ref[...].astype(BF16), w_ref[...]).astype(out_ref.dtype)


def _latent_out(olat2d, w_uv_bf):
    n = olat2d.shape[0]
    return pl.pallas_call(
        _latent_out_kernel,
        out_shape=jax.ShapeDtypeStruct((n, MLA_HEADS * HEAD_DIM), BF16),
        grid=(MLA_HEADS,),
        in_specs=[pl.BlockSpec((n, KV_LORA), lambda h: (0, h)),
                  pl.BlockSpec((None, KV_LORA, HEAD_DIM), lambda h: (h, 0, 0))],
        out_specs=pl.BlockSpec((n, HEAD_DIM), lambda h: (0, h)),
        compiler_params=_cparams(("arbitrary",), 4 * n * KV_LORA * 4),
        name="latent_out",
    )(olat2d, w_uv_bf)


def _pack_bf16_pairs(a, b):
    ua = pltpu.bitcast(a.astype(BF16).astype(F32), U32) & jnp.uint32(0xFFFF0000)
    ub = pltpu.bitcast(b.astype(BF16).astype(F32), U32) >> jnp.uint32(16)
    return ua | ub


def _unpack_bf16_pairs(u):
    a = pltpu.bitcast(u & jnp.uint32(0xFFFF0000), F32)
    b = pltpu.bitcast(u << jnp.uint32(16), F32)
    return a, b


def _outproj_kernel(ad_ref, am_ref, x_ref, gt_ref, sh_ref, sc_ref, g2_ref, wo_ref, wr_ref, br_ref, tri_ref, cin_ref,
                    x1_ref, h2_ref, slab_ref, cout_ref, carry_ref):
    i = pl.program_id(0)

    @pl.when(i == 0)
    def _():
        carry_ref[...] = cin_ref[...]

    half = wo_ref.shape[0] // 2
    mix = _dot(ad_ref[...], wo_ref[:half, :]) + _dot(am_ref[...], wo_ref[half:, :])
    x1 = x_ref[...] + gt_ref[...] * mix
    x1_ref[...] = x1
    h = x1 * lax.rsqrt(jnp.mean(x1 * x1, axis=-1, keepdims=True) + RMS_EPS) * g2_ref[...]
    h = h * (1.0 + sc_ref[...]) + sh_ref[...]
    d = h.shape[1]
    h2_ref[...] = _pack_bf16_pairs(h[:, :d // 2], h[:, d // 2:])

    h_hi = h.astype(BF16)
    h_lo = (h - h_hi.astype(F32)).astype(BF16)
    r2 = _dot(jnp.concatenate([h_hi, h_lo], axis=1), wr_ref[...])
    ne = r2.shape[1] // 2
    logits = r2[:, :ne] + r2[:, ne:] + br_ref[...]
    tm = logits.shape[0]
    lane = _lane_iota((tm, ne)).astype(F32)
    slab_lane = _lane_iota((tm, LANES))
    carry = carry_ref[...]
    work = logits
    onehots, vals, idxs = [], [], []
    for _ in range(TOP_K):
        mx = jnp.max(work, axis=-1, keepdims=True)
        ix = jnp.min(jnp.where(work == mx, lane, float(ne)), axis=-1, keepdims=True)
        oh = lane == ix
        onehots.append(oh)
        vals.append(mx)
        idxs.append(ix)
        work = jnp.where(oh, -jnp.inf, work)
    cnt = sum(jnp.where(oh, 1.0, 0.0) for oh in onehots)
    before = _dot(tri_ref[...], cnt.astype(BF16)) + carry
    exps = [jnp.exp(v - vals[0]) for v in vals]
    den = sum(exps)
    slab = jnp.zeros((tm, LANES), F32)
    for k in range(TOP_K):
        rank = jnp.sum(jnp.where(onehots[k], before, 0.0), axis=-1, keepdims=True)
        slab = jnp.where(slab_lane == k, idxs[k], slab)
        slab = jnp.where(slab_lane == TOP_K + k, rank, slab)
        slab = jnp.where(slab_lane == 2 * TOP_K + k, exps[k] / den, slab)
    slab_ref[...] = slab
    new_carry = carry + jnp.sum(cnt, axis=0, keepdims=True)
    carry_ref[...] = new_carry
    cout_ref[...] = new_carry


def _out_project(attn_d, attn_m, x2d, gate, shift, scale, consts, count_in, *, rows_per_mod, tm, name):
    n, d = x2d.shape
    steps = n // tm
    mod_rows = gate.shape[1]
    mod_spec = pl.BlockSpec((None, mod_rows, d), lambda i: (i * tm // rows_per_mod, 0, 0))

    def full(a):
        nd = a.ndim
        return pl.BlockSpec(a.shape, lambda i: (0,) * nd)

    in_specs = [pl.BlockSpec((tm, d // 2), lambda i: (i, 0)), pl.BlockSpec((tm, d // 2), lambda i: (i, 0)),
                pl.BlockSpec((tm, d), lambda i: (i, 0)), mod_spec, mod_spec, mod_spec] + [full(a) for a in consts] + [full(count_in)]
    out_shape = [jax.ShapeDtypeStruct((n, d), F32), jax.ShapeDtypeStruct((n, d // 2), U32),
                 jax.ShapeDtypeStruct((n, LANES), F32), jax.ShapeDtypeStruct(count_in.shape, F32)]
    out_specs = [pl.BlockSpec((tm, d), lambda i: (i, 0)), pl.BlockSpec((tm, d // 2), lambda i: (i, 0)),
                 pl.BlockSpec((tm, LANES), lambda i: (i, 0)), full(count_in)]
    weight_bytes = sum(int(a.size) * a.dtype.itemsize for a in consts)
    return pl.pallas_call(
        _outproj_kernel, out_shape=out_shape, grid=(steps,), in_specs=in_specs, out_specs=out_specs,
        scratch_shapes=[pltpu.VMEM(count_in.shape, F32)],
        compiler_params=_cparams(("arbitrary",), 2 * weight_bytes + 16 * tm * d * 4),
        name=name,
    )(attn_d, attn_m, x2d, gate, shift, scale, *consts, count_in)


def _dispatch_kernel(dest_ref, hp_ref, hs_ref, xs_in, xs_out, sem, *, tm, prompt_steps):
    del xs_in

    def scatter_rows(h_ref):
        def row_copy(r, k):
            return pltpu.make_async_copy(h_ref.at[pl.ds(r, 1), :],
                                         xs_out.at[pl.ds(dest_ref[r * TOP_K + k], 1), :], sem)

        def issue(r, c):
            for k in range(TOP_K):
                row_copy(r, k).start()
            return c

        def drain(r, c):
            for k in range(TOP_K):
                row_copy(r, k).wait()
            return c

        lax.fori_loop(0, tm, issue, 0, unroll=DMA_UNROLL)
        lax.fori_loop(0, tm, drain, 0, unroll=DMA_UNROLL)

    @pl.when(pl.program_id(0) < prompt_steps)
    def _():
        scatter_rows(hp_ref)

    @pl.when(pl.program_id(0) >= prompt_steps)
    def _():
        scatter_rows(hs_ref)


def _dispatch(h2_p, h2_s, dest_flat, xs_zero):
    tm = TM_TOK
    w = h2_p.shape[1]
    p_steps, s_steps = h2_p.shape[0] // tm, h2_s.shape[0] // tm
    return pl.pallas_call(
        functools.partial(_dispatch_kernel, tm=tm, prompt_steps=p_steps),
        out_shape=jax.ShapeDtypeStruct(xs_zero.shape, U32),
        grid=(p_steps + s_steps,),
        in_specs=[pl.BlockSpec((tm * TOP_K,), lambda i: (i,), memory_space=pltpu.SMEM),
                  pl.BlockSpec((tm, w), lambda i: (jnp.minimum(i, p_steps - 1), 0)),
                  pl.BlockSpec((tm, w), lambda i: (jnp.maximum(i - p_steps, 0), 0)),
                  pl.BlockSpec(memory_space=pl.ANY)],
        out_specs=pl.BlockSpec(memory_space=pl.ANY),
        scratch_shapes=[pltpu.SemaphoreType.DMA(())],
        input_output_aliases={3: 0},
        compiler_params=pltpu.CompilerParams(dimension_semantics=("arbitrary",), has_side_effects=True),
        name="moe_dispatch",
    )(dest_flat, h2_p, h2_s, xs_zero)


def _expert_changed(be_ref, i):
    prev = be_ref[jnp.maximum(i - 1, 0)]
    return jnp.logical_or(i == 0, be_ref[i] != prev)


def _gate_up_kernel(be_ref, nu_ref, xs_ref, wg_ref, wu_ref, bg_ref, bu_ref, act_ref, wg_bf, wu_bf):
    i = pl.program_id(1)

    @pl.when(jnp.logical_and(i < nu_ref[0], _expert_changed(be_ref, i)))
    def _():
        wg_bf[...] = wg_ref[...].astype(BF16)
        wu_bf[...] = wu_ref[...].astype(BF16)

    @pl.when(i < nu_ref[0])
    def _():
        a, b = _unpack_bf16_pairs(xs_ref[...])
        x = jnp.concatenate([a, b], axis=1).astype(BF16)
        gate = jnp.minimum(_dot(x, wg_bf[...]) + bg_ref[...], SWIGLU_LIMIT)
        up = jnp.clip(_dot(x, wu_bf[...]) + bu_ref[...], -SWIGLU_LIMIT, SWIGLU_LIMIT)
        act_ref[...] = ((up + 1.0) * gate * jax.nn.sigmoid(SWIGLU_ALPHA * gate)).astype(act_ref.dtype)

    @pl.when(i >= nu_ref[0])
    def _():
        act_ref[...] = jnp.zeros_like(act_ref)


def _down_kernel(be_ref, nu_ref, act_ref, wa_ref, wb_ref, ba_ref, bb_ref, y_ref, wa_bf, wb_bf):
    i = pl.program_id(1)

    @pl.when(jnp.logical_and(i < nu_ref[0], _expert_changed(be_ref, i)))
    def _():
        wa_bf[...] = wa_ref[...].astype(BF16)
        wb_bf[...] = wb_ref[...].astype(BF16)

    @pl.when(i < nu_ref[0])
    def _():
        act = act_ref[...]
        ya = _dot(act, wa_bf[...]) + ba_ref[...]
        yb = _dot(act, wb_bf[...]) + bb_ref[...]
        y_ref[...] = _pack_bf16_pairs(ya, yb)

    @pl.when(i >= nu_ref[0])
    def _():
        y_ref[...] = jnp.zeros_like(y_ref)


def _experts(xs, block_exp, n_used, w_gu, b_gu, w_down, b_down):
    n_rows, half_d = xs.shape
    d = 2 * half_d
    ne, _, two_f = w_gu.shape
    f = two_f // 2
    tb, tn, tn2 = TB_MOE, TN_GU, TN_DOWN
    n_blk = n_rows // tb
    b_gu3 = b_gu.reshape(ne, 1, two_f)
    b_down3 = b_down.reshape(ne, 1, d)

    def blk(i, nu):
        return jnp.minimum(i, nu[0] - 1)

    up_off = f // tn
    gs1 = pltpu.PrefetchScalarGridSpec(
        num_scalar_prefetch=2, grid=(f // tn, n_blk),
        in_specs=[pl.BlockSpec((tb, half_d), lambda j, i, be, nu: (blk(i, nu), 0)),
                  pl.BlockSpec((None, d, tn), lambda j, i, be, nu: (be[blk(i, nu)], 0, j)),
                  pl.BlockSpec((None, d, tn), lambda j, i, be, nu: (be[blk(i, nu)], 0, j + up_off)),
                  pl.BlockSpec((None, 1, tn), lambda j, i, be, nu: (be[blk(i, nu)], 0, j)),
                  pl.BlockSpec((None, 1, tn), lambda j, i, be, nu: (be[blk(i, nu)], 0, j + up_off))],
        out_specs=pl.BlockSpec((tb, tn), lambda j, i, be, nu: (i, j)),
        scratch_shapes=[pltpu.VMEM((d, tn), BF16), pltpu.VMEM((d, tn), BF16)])
    act = pl.pallas_call(
        _gate_up_kernel, out_shape=jax.ShapeDtypeStruct((n_rows, f), BF16), grid_spec=gs1,
        compiler_params=_cparams(("arbitrary", "arbitrary"), 4 * d * tn * 4 + 2 * d * tn * 2 + 8 * tb * d * 4),
        name="moe_gate_up",
    )(block_exp, n_used, xs, w_gu, w_gu, b_gu3, b_gu3)

    hi_off = half_d // tn2
    gs2 = pltpu.PrefetchScalarGridSpec(
        num_scalar_prefetch=2, grid=(half_d // tn2, n_blk),
        in_specs=[pl.BlockSpec((tb, f), lambda j, i, be, nu: (blk(i, nu), 0)),
                  pl.BlockSpec((None, f, tn2), lambda j, i, be, nu: (be[blk(i, nu)], 0, j)),
                  pl.BlockSpec((None, f, tn2), lambda j, i, be, nu: (be[blk(i, nu)], 0, j + hi_off)),
                  pl.BlockSpec((None, 1, tn2), lambda j, i, be, nu: (be[blk(i, nu)], 0, j)),
                  pl.BlockSpec((None, 1, tn2), lambda j, i, be, nu: (be[blk(i, nu)], 0, j + hi_off))],
        out_specs=pl.BlockSpec((tb, tn2), lambda j, i, be, nu: (i, j)),
        scratch_shapes=[pltpu.VMEM((f, tn2), BF16), pltpu.VMEM((f, tn2), BF16)])
    return pl.pallas_call(
        _down_kernel, out_shape=jax.ShapeDtypeStruct((n_rows, half_d), U32), grid_spec=gs2,
        compiler_params=_cparams(("arbitrary", "arbitrary"), 4 * f * tn2 * 4 + 2 * f * tn2 * 2 + 8 * tb * f * 2),
        name="moe_down",
    )(block_exp, n_used, act, w_down, w_down, b_down3, b_down3)


def _combine_kernel(dest_ref, x1_ref, gt_ref, slab_ref, yb_ref, y_ref, buf, sem, *, tm):
    def row_copy(r, k):
        return pltpu.make_async_copy(yb_ref.at[pl.ds(dest_ref[r * TOP_K + k], 1), :],
                                     buf.at[k, pl.ds(r, 1), :], sem)

    def issue(r, c):
        for k in range(TOP_K):
            row_copy(r, k).start()
        return c

    def drain(r, c):
        for k in range(TOP_K):
            row_copy(r, k).wait()
        return c

    lax.fori_loop(0, tm, issue, 0, unroll=DMA_UNROLL)
    lax.fori_loop(0, tm, drain, 0, unroll=DMA_UNROLL)
    slab = slab_ref[...]
    acc_a = None
    for k in range(TOP_K):
        a, b = _unpack_bf16_pairs(buf[k])
        g = slab[:, 2 * TOP_K + k:2 * TOP_K + k + 1]
        acc_a = g * a if acc_a is None else acc_a + g * a
        acc_b = g * b if k == 0 else acc_b + g * b
    moe = jnp.concatenate([acc_a, acc_b], axis=1)
    y_ref[...] = x1_ref[...] + gt_ref[...] * moe


def _combine(x1, gate, slab, dest_flat, yb, *, rows_per_mod, tm):
    n, d = x1.shape
    mod_rows = gate.shape[1]
    return pl.pallas_call(
        functools.partial(_combine_kernel, tm=tm),
        out_shape=jax.ShapeDtypeStruct((n, d), F32),
        grid=(n // tm,),
        in_specs=[pl.BlockSpec((tm * TOP_K,), lambda i: (i,), memory_space=pltpu.SMEM),
                  pl.BlockSpec((tm, d), lambda i: (i, 0)),
                  pl.BlockSpec((None, mod_rows, d), lambda i: (i * tm // rows_per_mod, 0, 0)),
                  pl.BlockSpec((tm, LANES), lambda i: (i, 0)),
                  pl.BlockSpec(memory_space=pl.ANY)],
        out_specs=pl.BlockSpec((tm, d), lambda i: (i, 0)),
        scratch_shapes=[pltpu.VMEM((TOP_K, tm, d // 2), U32), pltpu.SemaphoreType.DMA(())],
        compiler_params=_cparams(("arbitrary",), 8 * tm * d * 4),
        name="moe_combine",
    )(dest_flat, x1, gate, slab, yb)


def _rope_table(pos):
    inv = ROPE_THETA ** (-jnp.arange(0, MLA_ROPE, 2, dtype=F32) / MLA_ROPE)
    ang = pos.astype(F32)[:, None] * inv[None, :]
    c, s = jnp.cos(ang), jnp.sin(ang)
    return jnp.concatenate([c, c, -s, s], axis=-1)


def _swap_halves(a):
    h = a.shape[-1] // 2
    return jnp.concatenate([a[..., h:], a[..., :h]], axis=-1)


def kernel(x_prompt, x_sample, cache_dk, cache_dv, cache_ckv, cache_kpe, page_table, c_prompt, c_sample, norm1_g, norm2_g, w_ada, b_ada, w_in, diff_qn_g, diff_kn_g, diff_lambda, diff_subln_g, mla_qa_g, w_qb, mla_qn_g, mla_kva_g, mla_kpe_g, w_uk, w_uv, w_out, w_router, b_router, w_gu, b_gu, w_down, b_down):
    depth = w_in.shape[0]
    assert depth == 1, "single-layer trunk"
    batch, seq, d = x_prompt.shape
    n_seq, dec_seq, _ = x_sample.shape
    assert dec_seq == 1
    n_pool = cache_ckv.shape[1]
    n_pages = page_table.shape[1]
    past_len = n_pages * PAGE_SIZE
    n_p = batch * seq
    n_all = n_p + n_seq
    lam_init = 0.8 - 0.6 * math.exp(-0.3 * 0)
    assert seq % TM_PROJ == 0 and seq % TQ_MLA == 0 and seq % TK_ATTN == 0 and n_p % TM_TOK == 0 and n_seq % TM_TOK == 0

    w_in0 = w_in[0]
    w_in_ext = jnp.concatenate([w_in0, _swap_halves(w_in0[:, O_KPE:])], axis=1).astype(BF16)
    gmat = (jnp.arange(LANES)[:, None] // DIFF_DH == jnp.arange(LANES)[None, :] // DIFF_DH).astype(BF16)
    gdq = (jnp.tile(diff_qn_g[0].reshape(-1), DIFF_HEADS) * (DIFF_DH ** -0.5)).reshape(1, -1)
    gdk = jnp.tile(diff_kn_g[0].reshape(-1), DIFF_KV_HEADS).reshape(1, -1)
    gkpe = jnp.concatenate([mla_kpe_g[0], _swap_halves(mla_kpe_g[0])]).reshape(1, -1)
    qscale = MLA_QK_DIM ** -0.5
    gqn = (mla_qn_g[0, :MLA_NOPE] * qscale).reshape(1, -1)
    gq_pe = mla_qn_g[0, MLA_NOPE:]
    gqp = (jnp.concatenate([gq_pe, _swap_halves(gq_pe)]) * qscale).reshape(1, -1)
    wqb3 = w_qb[0].reshape(Q_LORA, MLA_HEADS, MLA_QK_DIM)
    wqb_pe = wqb3[:, :, MLA_NOPE:]
    wqb_ext = jnp.concatenate([wqb3[:, :, :MLA_NOPE].reshape(Q_LORA, -1),
                               jnp.concatenate([wqb_pe, _swap_halves(wqb_pe)], axis=-1).reshape(Q_LORA, -1)],
                              axis=1).astype(BF16)
    w_kvup = jnp.concatenate([jnp.transpose(w_uk[0], (1, 0, 2)).reshape(KV_LORA, -1),
                              jnp.transpose(w_uv[0], (1, 0, 2)).reshape(KV_LORA, -1)], axis=1).astype(BF16)
    w_uk_t = jnp.transpose(w_uk[0], (0, 2, 1)).astype(BF16)
    w_uv_bf = w_uv[0].astype(BF16)
    proj_consts = [norm1_g[0].reshape(1, -1), w_in_ext, gmat, gdq, gdk, mla_qa_g[0].reshape(1, -1),
                   mla_kva_g[0].reshape(1, -1), gkpe, wqb_ext, gqn, gqp]
    lam_v = diff_lambda[0]
    gsub = diff_subln_g[0].reshape(1, -1)

    mod = _modulation(jnp.concatenate([c_prompt, c_sample], axis=0), w_ada[0], b_ada[0])
    mod_p = mod[:batch].reshape(batch, 6, 1, d)
    mod_s = mod[batch:].reshape(n_seq, 6, d)
    sh1_p, sc1_p, gt1_p, sh2_p, sc2_p, gt2_p = [mod_p[:, t] for t in range(6)]
    sh1_s, sc1_s, gt1_s, sh2_s, sc2_s, gt2_s = [mod_s[:, t][None] for t in range(6)]

    xp2 = x_prompt.reshape(n_p, d)
    xs2 = x_sample.reshape(n_seq, d)
    tab_p = _rope_table(jnp.arange(seq, dtype=I32))
    tab_s = _rope_table(jnp.full((1,), past_len, I32))
    (qds, dk_p, dv_p, ckv_p, kpe_p, qmla, kmla, vmla) = _project(
        xp2, sh1_p, sc1_p, tab_p, proj_consts, w_kvup, sample=False, rows_per_mod=seq, tm=TM_PROJ)
    (qd_s, dk_s, dv_s, ckv_s, kpe_s, qlat_s, qpe_s) = _project(
        xs2, sh1_s, sc1_s, tab_s, proj_consts, w_uk_t, sample=True, rows_per_mod=n_seq, tm=n_seq)

    od_p = _diff_prompt_attention(qds, dk_p, dv_p, lam_v, gsub, batch=batch, seq=seq, lam_init=lam_init)
    om_p = _mla_prompt_attention(qmla, kmla, vmla, batch=batch, seq=seq)

    c_ckv = cache_ckv.reshape(n_pool, PAGE_SIZE, KV_LORA)
    c_kpe = jnp.transpose(cache_kpe, (0, 1, 3, 2)).reshape(n_pool, MLA_ROPE, PAGE_SIZE)
    c_dk = jnp.transpose(cache_dk, (0, 1, 3, 4, 5, 2)).reshape(n_pool, DIFF_K_COLS, PAGE_SIZE)
    c_dv = cache_dv.reshape(n_pool, PAGE_SIZE * DIFF_KV_HEADS, HEAD_DIM)
    q5 = qd_s.reshape(n_seq, DIFF_KV_HEADS, DIFF_GROUP, 2, DIFF_DH)
    eye_kv = jnp.eye(DIFF_KV_HEADS, dtype=F32)
    eye_m = jnp.eye(2, dtype=F32)
    qdm = jnp.einsum("skgmd,kK,mM->skmgKMd", q5, eye_kv, eye_m).reshape(n_seq, 2 * DIFF_HEADS, DIFF_K_COLS)
    head_id = jnp.arange(2 * DIFF_HEADS) // (2 * DIFF_GROUP) * DIFF_GROUP + jnp.arange(2 * DIFF_HEADS) % DIFF_GROUP
    slopes = jnp.broadcast_to((2.0 ** (-(head_id + 1).astype(F32)))[:, None], (2 * DIFF_HEADS, LANES))
    qpe3 = qpe_s.reshape(n_seq, MLA_HEADS, LANES)[:, :, :MLA_ROPE]
    olat_s, od_s = _decode_attention(
        page_table, lam_v, gsub, slopes, qlat_s.reshape(n_seq, MLA_HEADS, KV_LORA), qpe3,
        ckv_s.reshape(n_seq, 1, KV_LORA), kpe_s.reshape(n_seq, 1, MLA_ROPE), qdm,
        dk_s.reshape(n_seq, 1, DIFF_K_COLS), dv_s.reshape(n_seq, 1, DIFF_V_COLS),
        c_ckv, c_kpe, c_dk, c_dv, lam_init=lam_init)
    om_s = _latent_out(olat_s.reshape(n_seq, MLA_HEADS * KV_LORA), w_uv_bf)
    od_s = od_s.reshape(n_seq, DIFF_HEADS * HEAD_DIM).astype(BF16)

    tri = (jnp.arange(TM_PROJ)[:, None] > jnp.arange(TM_PROJ)[None, :]).astype(BF16)
    wr_hi = w_router[0].astype(BF16)
    wr_lo = (w_router[0] - wr_hi.astype(F32)).astype(BF16)
    wr_split = jnp.concatenate([jnp.concatenate([wr_hi, wr_lo], axis=1),
                                jnp.concatenate([wr_hi, jnp.zeros_like(wr_lo)], axis=1)], axis=0)
    out_consts = [norm2_g[0].reshape(1, -1), w_out[0].astype(BF16), wr_split, b_router[0].reshape(1, -1)]
    zero_counts = jnp.zeros((1, N_EXPERTS), F32)
    x1_p, h2_p, slab_p, counts_p = _out_project(
        od_p, om_p, xp2, gt1_p, sh2_p, sc2_p, out_consts + [tri], zero_counts,
        rows_per_mod=seq, tm=TM_PROJ, name="out_proj_router_prompt")
    x1_s, h2_s, slab_s, counts = _out_project(
        od_s, om_s, xs2, gt1_s, sh2_s, sc2_s, out_consts + [tri[:n_seq, :n_seq]], counts_p,
        rows_per_mod=n_seq, tm=n_seq, name="out_proj_router_sample")

    tb = TB_MOE
    n_blk = (n_all * TOP_K + N_EXPERTS * (tb - 1)) // tb + 1
    cnt = counts[0].astype(I32)
    padded = (cnt + tb - 1) // tb * tb
    pad_end = jnp.cumsum(padded)
    pad_start = pad_end - padded
    slab = jnp.concatenate([slab_p, slab_s], axis=0)
    e_idx = slab[:, :TOP_K].astype(I32)
    rank = slab[:, TOP_K:2 * TOP_K].astype(I32)
    experts = jnp.arange(N_EXPERTS, dtype=I32)
    start_of = jnp.sum(jnp.where(e_idx[..., None] == experts, pad_start, 0), axis=-1)
    dest = (start_of + rank).reshape(-1)
    n_used = (pad_end[-1] // tb).astype(I32).reshape(1)
    blk_start = jnp.arange(n_blk, dtype=I32) * tb
    block_exp = jnp.minimum(jnp.sum((pad_end[None, :] <= blk_start[:, None]).astype(I32), axis=1), N_EXPERTS - 1)

    xs = _dispatch(h2_p, h2_s, dest, jnp.zeros((n_blk * tb, d // 2), U32))
    yb = _experts(xs, block_exp, n_used, w_gu[0], b_gu[0], w_down[0], b_down[0])
    y_p = _combine(x1_p, gt2_p, slab_p, dest[:n_p * TOP_K], yb, rows_per_mod=seq, tm=TM_TOK)
    y_s = _combine(x1_s, gt2_s, slab_s, dest[n_p * TOP_K:], yb, rows_per_mod=n_seq, tm=TM_TOK)

    def kv_outputs(dk, dv, ckv, kpe, b, t):
        return (dk.reshape(1, b, t, DIFF_KV_HEADS, 2, DIFF_DH), dv.reshape(1, b, t, DIFF_KV_HEADS, HEAD_DIM),
                ckv.reshape(1, b, t, KV_LORA), kpe.reshape(1, b, t, MLA_ROPE))

    return ((y_p.reshape(batch, seq, d), y_s.reshape(n_seq, 1, d))
            + kv_outputs(dk_p, dv_p, ckv_p, kpe_p, batch, seq) + kv_outputs(dk_s, dv_s, ckv_s, kpe_s, n_seq, 1))
```

```python
import functools
import math

import jax
import jax.numpy as jnp
from jax import lax
from jax.experimental import pallas as pl
from jax.experimental.pallas import tpu as pltpu

F32 = jnp.float32
BF16 = jnp.bfloat16
U32 = jnp.uint32
I32 = jnp.int32

LANES = 128
V7X_VMEM_BYTES = 64 * 1024 * 1024

HEAD_DIM = 128
DIFF_KV_HEADS = 2
DIFF_GROUP = 4
DIFF_HEADS = DIFF_KV_HEADS * DIFF_GROUP
DIFF_DH = 64
MLA_HEADS = 8
Q_LORA = 512
KV_LORA = 512
MLA_NOPE = 128
MLA_ROPE = 64
MLA_QK_DIM = MLA_NOPE + MLA_ROPE
ROPE_THETA = 10000.0
N_EXPERTS = 32
TOP_K = 4
SWIGLU_LIMIT = 7.0
SWIGLU_ALPHA = 1.702
PAGE_SIZE = 128
RMS_EPS = 1e-6
NEG_INF = -1e30

DIFF_Q_COLS = DIFF_HEADS * 2 * DIFF_DH
DIFF_K_COLS = DIFF_KV_HEADS * 2 * DIFF_DH
DIFF_V_COLS = DIFF_KV_HEADS * HEAD_DIM
O_DQ = 0
O_DK = O_DQ + DIFF_Q_COLS
O_DV = O_DK + DIFF_K_COLS
O_MQ = O_DV + DIFF_V_COLS
O_CKV = O_MQ + Q_LORA
O_KPE = O_CKV + KV_LORA
IN_COLS = O_KPE + MLA_ROPE
IN_COLS_EXT = IN_COLS + MLA_ROPE

TM_PROJ = 256
TQ_DIFF = 256
TQ_MLA = 256
CHAIN_MLA = 128
TK_ATTN = 256
TM_TOK = 128
DMA_UNROLL = 8
TB_MOE = 512
TN_GU = 1024
TN_DOWN = 1024
TN_ADA = 1024
PAGES_PER_CHUNK = 16


def _vmem_limit(nbytes):
    return int(min(max(nbytes * 5 // 4, 16 * 1024 * 1024), V7X_VMEM_BYTES - 8 * 1024 * 1024))


def _cparams(sem, vmem_bytes):
    return pltpu.CompilerParams(dimension_semantics=sem, vmem_limit_bytes=_vmem_limit(vmem_bytes))


def _dot(a, b):
    return jnp.dot(a, b, preferred_element_type=F32)


def _dot_nt(a, b):
    return lax.dot_general(a, b, (((1,), (1,)), ((), ())), preferred_element_type=F32)


def _lane_iota(shape):
    return lax.broadcasted_iota(I32, shape, len(shape) - 1)


def _mod_kernel(c_ref, w_ref, b_ref, o_ref):
    c = c_ref[...]
    a = (c * jax.nn.sigmoid(c)).astype(BF16)
    o_ref[...] = _dot(a, w_ref[...].astype(BF16)) + b_ref[...]


def _modulation(c_all, w_ada, b_ada):
    r, d = c_all.shape
    n = w_ada.shape[1]
    tn = TN_ADA
    return pl.pallas_call(
        _mod_kernel,
        out_shape=jax.ShapeDtypeStruct((r, n), F32),
        grid=(n // tn,),
        in_specs=[pl.BlockSpec((r, d), lambda j: (0, 0)),
                  pl.BlockSpec((d, tn), lambda j: (0, j)),
                  pl.BlockSpec((1, tn), lambda j: (0, j))],
        out_specs=pl.BlockSpec((r, tn), lambda j: (0, j)),
        compiler_params=_cparams(("arbitrary",), 2 * d * tn * 4 + d * tn * 2 + 4 * r * d * 4),
        name="adaln_modulation",
    )(c_all, w_ada, b_ada.reshape(1, n))


def _group64_sumsq(x, gmat):
    x2 = (x * x).astype(BF16)
    outs = [_dot(x2[:, j * LANES:(j + 1) * LANES], gmat) for j in range(x.shape[1] // LANES)]
    return outs[0] if len(outs) == 1 else jnp.concatenate(outs, axis=1)


def _rope_pair_block(blk, tab):
    r = blk * tab
    return r + pltpu.roll(r, 64, 1)


def _proj_kernel(x_ref, sh_ref, sc_ref, tab_ref, g1_ref, win_ref, gmat_ref, gdq_ref, gdk_ref, gqa_ref,
                 gckv_ref, gkpe_ref, wqb_ref, gqn_ref, gqp_ref, wup_ref, *out_refs, sample):
    x = x_ref[...]
    h = x * lax.rsqrt(jnp.mean(x * x, axis=-1, keepdims=True) + RMS_EPS) * g1_ref[...]
    h = h * (1.0 + sc_ref[...]) + sh_ref[...]
    z = _dot(h.astype(BF16), win_ref[...])
    gmat = gmat_ref[...]
    tab = tab_ref[...]
    lane = _lane_iota((1, LANES))
    low_half = lane < 64

    dq = z[:, O_DQ:O_DK]
    qd = dq * lax.rsqrt(_group64_sumsq(dq, gmat) * (1.0 / DIFF_DH) + RMS_EPS) * gdq_ref[...]
    dk = z[:, O_DK:O_DV]
    kd = dk * lax.rsqrt(_group64_sumsq(dk, gmat) * (1.0 / DIFF_DH) + RMS_EPS) * gdk_ref[...]
    vd = z[:, O_DV:O_MQ]
    mq = z[:, O_MQ:O_CKV]
    ckv = z[:, O_CKV:O_KPE]
    kpb = z[:, O_KPE:IN_COLS_EXT]

    ckv_n = ckv * lax.rsqrt(jnp.mean(ckv * ckv, axis=-1, keepdims=True) + RMS_EPS) * gckv_ref[...]
    kp_ms = 0.5 * jnp.sum(kpb * kpb, axis=-1, keepdims=True) * (1.0 / MLA_ROPE)
    kpn = kpb * lax.rsqrt(kp_ms + RMS_EPS) * gkpe_ref[...]
    kr = _rope_pair_block(kpn, tab)

    mqn = (mq * lax.rsqrt(jnp.mean(mq * mq, axis=-1, keepdims=True) + RMS_EPS) * gqa_ref[...]).astype(BF16)
    qm = _dot(mqn, wqb_ref[...])
    gqn = gqn_ref[...]
    gqp = gqp_ref[...]
    q_nope, q_rope = [], []
    for hh in range(MLA_HEADS):
        n_h = qm[:, hh * LANES:(hh + 1) * LANES]
        p_h = qm[:, (MLA_HEADS + hh) * LANES:(MLA_HEADS + hh + 1) * LANES]
        ms = (jnp.sum(n_h * n_h, axis=-1, keepdims=True)
              + 0.5 * jnp.sum(p_h * p_h, axis=-1, keepdims=True)) * (1.0 / MLA_QK_DIM)
        r = lax.rsqrt(ms + RMS_EPS)
        q_nope.append(n_h * r * gqn)
        q_rope.append(_rope_pair_block(p_h * r * gqp, tab))

    if sample:
        qd_ref, dk_ref, dv_ref, ckv_ref, kpe_ref, qlat_ref, qpe_ref = out_refs
        qd_ref[...] = qd
        qlat_ref[...] = jnp.concatenate(
            [_dot(q_nope[hh].astype(BF16), wup_ref[hh]) for hh in range(MLA_HEADS)], axis=1)
        qpe_ref[...] = jnp.concatenate(q_rope, axis=1)
    else:
        qds_ref, dk_ref, dv_ref, ckv_ref, kpe_ref, qmla_ref, kmla_ref, vmla_ref = out_refs
        blocks = []
        for c in range(DIFF_HEADS):
            blk = qd[:, c * LANES:(c + 1) * LANES]
            blocks.append(jnp.where(low_half, blk, 0.0))
            blocks.append(jnp.where(low_half, 0.0, blk))
        qds_ref[...] = jnp.concatenate(blocks, axis=1).astype(BF16)
        kvup = _dot(ckv_n.astype(BF16), wup_ref[...])
        kr_lo = jnp.where(low_half, kr, 0.0)
        qb, kb = [], []
        for hh in range(MLA_HEADS):
            qb += [q_nope[hh], q_rope[hh]]
            kb += [kvup[:, hh * LANES:(hh + 1) * LANES], kr_lo]
        qmla_ref[...] = jnp.concatenate(qb, axis=1).astype(BF16)
        kmla_ref[...] = jnp.concatenate(kb, axis=1).astype(BF16)
        vmla_ref[...] = kvup[:, MLA_HEADS * LANES:].astype(BF16)
    if sample:
        dk_ref[...] = kd
        kpe_ref[...] = kr[:, :MLA_ROPE]
    else:
        dk_ref[...] = kd.T
        kpe_ref[...] = kr.T[:MLA_ROPE, :]
    dv_ref[...] = vd
    ckv_ref[...] = ckv_n


def _project(x2d, shift, scale, tab, consts, wup, *, sample, rows_per_mod, tm):
    n, d = x2d.shape
    steps = n // tm
    mod_rows = shift.shape[1]
    tab_rows = tab.shape[0]
    tab_block = tm if tab_rows > 1 else 1
    tab_steps = max(tab_rows // tm, 1)
    mod_spec = pl.BlockSpec((None, mod_rows, d), lambda i: (i * tm // rows_per_mod, 0, 0))
    tab_spec = pl.BlockSpec((tab_block, LANES), lambda i: (i % tab_steps, 0))

    def full(a):
        nd = a.ndim
        return pl.BlockSpec(a.shape, lambda i: (0,) * nd)

    def rows(cols, dt):
        return jax.ShapeDtypeStruct((n, cols), dt), pl.BlockSpec((tm, cols), lambda i: (i, 0))

    if sample:
        outs = [rows(DIFF_Q_COLS, F32), rows(DIFF_K_COLS, F32), rows(DIFF_V_COLS, F32), rows(KV_LORA, F32),
                rows(MLA_ROPE, F32), rows(MLA_HEADS * KV_LORA, F32), rows(MLA_HEADS * LANES, F32)]
    else:
        per_seq = rows_per_mod // tm

        def cols_t(width):
            return (jax.ShapeDtypeStruct((n // rows_per_mod, width, rows_per_mod), F32),
                    pl.BlockSpec((None, width, tm), lambda i: (i // per_seq, 0, i % per_seq)))

        outs = [rows(2 * DIFF_Q_COLS, BF16), cols_t(DIFF_K_COLS), rows(DIFF_V_COLS, F32), rows(KV_LORA, F32),
                cols_t(MLA_ROPE), rows(2 * MLA_HEADS * LANES, BF16), rows(2 * MLA_HEADS * LANES, BF16),
                rows(MLA_HEADS * LANES, BF16)]
    out_shape = [o[0] for o in outs]
    out_specs = [o[1] for o in outs]
    in_specs = [pl.BlockSpec((tm, d), lambda i: (i, 0)), mod_spec, mod_spec, tab_spec] + [full(a) for a in consts] + [full(wup)]
    weight_bytes = sum(int(a.size) * a.dtype.itemsize for a in consts) + int(wup.size) * wup.dtype.itemsize
    vmem = 2 * weight_bytes + 2 * tm * d * 4 + 12 * tm * IN_COLS_EXT * 4
    return pl.pallas_call(
        functools.partial(_proj_kernel, sample=sample),
        out_shape=out_shape, grid=(steps,), in_specs=in_specs, out_specs=out_specs,
        compiler_params=_cparams(("arbitrary",), vmem),
        name="in_proj_sample" if sample else "in_proj_prompt",
    )(x2d, shift, scale, tab, *consts, wup)


def _softmax_step(s, v_bf, m, l, acc):
    m_new = jnp.maximum(m, jnp.max(s, axis=-1, keepdims=True))
    alpha = jnp.exp(m - m_new)
    p = jnp.exp(s - m_new)
    l_new = alpha * l + jnp.sum(p, axis=-1, keepdims=True)
    acc_new = alpha * acc + _dot(p.astype(BF16), v_bf)
    return m_new, l_new, acc_new


def _diff_lambda(lv, lam_init):
    a = jnp.sum(lv[0:1, :] * lv[1:2, :], axis=-1, keepdims=True)
    b = jnp.sum(lv[2:3, :] * lv[3:4, :], axis=-1, keepdims=True)
    return jnp.exp(a) - jnp.exp(b) + lam_init


def _causal_blocks(i, tq, tk):
    assert tq % tk == 0 or tk % tq == 0
    return (i * tq) // tk, max(tq // tk, 1)


def _chain_stats(s, m_old, l_old):
    m_new = jnp.maximum(m_old, jnp.max(s, axis=-1, keepdims=True))
    alpha = jnp.exp(m_old - m_new)
    p = jnp.exp(s - m_new)
    return m_new, alpha * l_old + jnp.sum(p, axis=-1, keepdims=True), p.astype(BF16), alpha


def _chain_init(n_chains, rows):
    return tuple((jnp.full((rows, 1), NEG_INF, F32), jnp.zeros((rows, 1), F32), jnp.zeros((rows, HEAD_DIM), F32))
                 for _ in range(n_chains))


def _diff_prompt_kernel(lam_ref, gsub_ref, q_ref, kt_ref, v_ref, o_ref, *, tq, tk, lam_init):
    kvh = pl.program_id(1)
    i = pl.program_id(2)
    q = q_ref[...]
    nrow = 2 * DIFF_GROUP
    qs = jnp.concatenate([q[:, c * LANES:(c + 1) * LANES] for c in range(nrow)], axis=0)
    kv_scale = jnp.where(kvh == 0, 1.0, 2.0 ** (-DIFF_GROUP)).astype(F32)
    rel = (lax.broadcasted_iota(I32, (tq, tk), 0) - lax.broadcasted_iota(I32, (tq, tk), 1))

    def step(j, carry, masked):
        m, l, acc = carry
        start = pl.multiple_of(j * tk, tk)
        kt = kt_ref[:, pl.ds(start, tk)].astype(BF16)
        v = v_ref[pl.ds(start, tk), :].astype(BF16)
        s = _dot(qs, kt)
        dist = rel + (i * tq - j * tk)
        distf = dist.astype(F32)
        pieces = []
        for g in range(DIFF_GROUP):
            bias = (2.0 ** (-(g + 1))) * kv_scale * distf
            for mp in range(2):
                r0 = (g * 2 + mp) * tq
                sg = s[r0:r0 + tq] - bias
                if masked:
                    sg = jnp.where(dist >= 0, sg, NEG_INF)
                pieces.append(sg)
        s = jnp.concatenate(pieces, axis=0)
        return _softmax_step(s, v, m, l, acc)

    init = (jnp.full((nrow * tq, 1), NEG_INF, F32), jnp.zeros((nrow * tq, 1), F32),
            jnp.zeros((nrow * tq, HEAD_DIM), F32))
    n_full, n_diag = _causal_blocks(i, tq, tk)
    carry = lax.fori_loop(0, n_full, lambda j, c: step(j, c, False), init)
    for jj in range(n_diag):
        carry = step(n_full + jj, carry, True)
    m, l, acc = carry
    o = acc / l
    lam = _diff_lambda(lam_ref[...], lam_init)
    gsub = gsub_ref[...] * (1.0 - lam_init)
    outs = []
    for g in range(DIFF_GROUP):
        og = o[(2 * g) * tq:(2 * g + 1) * tq] - lam * o[(2 * g + 1) * tq:(2 * g + 2) * tq]
        og = og * lax.rsqrt(jnp.mean(og * og, axis=-1, keepdims=True) + RMS_EPS) * gsub
        outs.append(og)
    o_ref[...] = jnp.concatenate(outs, axis=1).astype(o_ref.dtype)


def _diff_prompt_attention(qds, dk_t, dv, lam_v, gsub, *, batch, seq, lam_init):
    tq, tk = TQ_DIFF, TK_ATTN
    nq = seq // tq
    nrow = 2 * DIFF_GROUP
    width = nrow * LANES
    vmem = 2 * (tq * width * 2 + 2 * seq * LANES * 4) + 6 * nrow * tq * tk * 4 + 3 * nrow * tq * LANES * 4
    return pl.pallas_call(
        functools.partial(_diff_prompt_kernel, tq=tq, tk=tk, lam_init=lam_init),
        out_shape=jax.ShapeDtypeStruct((batch * seq, DIFF_HEADS * HEAD_DIM), BF16),
        grid=(batch, DIFF_KV_HEADS, nq),
        in_specs=[pl.BlockSpec(lam_v.shape, lambda b, h, i: (0, 0)),
                  pl.BlockSpec(gsub.shape, lambda b, h, i: (0, 0)),
                  pl.BlockSpec((tq, width), lambda b, h, i: (b * nq + i, h)),
                  pl.BlockSpec((None, LANES, seq), lambda b, h, i: (b, h, 0)),
                  pl.BlockSpec((seq, HEAD_DIM), lambda b, h, i: (b, h))],
        out_specs=pl.BlockSpec((tq, DIFF_GROUP * HEAD_DIM), lambda b, h, i: (b * nq + i, h)),
        compiler_params=_cparams(("arbitrary", "arbitrary", "arbitrary"), vmem),
        name="diff_attn_prompt",
    )(lam_v, gsub, qds, dk_t, dv)


def _mla_prompt_kernel(q_ref, k_ref, v_ref, o_ref, *, tq, tk, chain):
    i = pl.program_id(1)
    qk = 2 * LANES
    per_head = tq // chain
    rel = (lax.broadcasted_iota(I32, (chain, tk), 0) - lax.broadcasted_iota(I32, (chain, tk), 1))

    def step(j, state, masked):
        start = pl.multiple_of(j * tk, tk)
        scores = []
        for hh in range(MLA_HEADS):
            k = k_ref[pl.ds(start, tk), hh * qk:(hh + 1) * qk]
            for r in range(per_head):
                scores.append(_dot_nt(q_ref[r * chain:(r + 1) * chain, hh * qk:(hh + 1) * qk], k))
        stats = []
        for hh in range(MLA_HEADS):
            for r in range(per_head):
                c = hh * per_head + r
                s = scores[c]
                if masked:
                    s = jnp.where(rel + (i * tq + r * chain - j * tk) >= 0, s, NEG_INF)
                stats.append(_chain_stats(s, state[c][0], state[c][1]))
        new = []
        for hh in range(MLA_HEADS):
            v = v_ref[pl.ds(start, tk), hh * HEAD_DIM:(hh + 1) * HEAD_DIM]
            for r in range(per_head):
                c = hh * per_head + r
                m_new, l_new, p, alpha = stats[c]
                new.append((m_new, l_new, alpha * state[c][2] + _dot(p, v)))
        return tuple(new)

    n_full, n_diag = _causal_blocks(i, tq, tk)
    state = lax.fori_loop(0, n_full, lambda j, st: step(j, st, False), _chain_init(MLA_HEADS * per_head, chain))
    for jj in range(n_diag):
        state = step(n_full + jj, state, True)
    for hh in range(MLA_HEADS):
        o_h = jnp.concatenate([state[hh * per_head + r][2] / state[hh * per_head + r][1] for r in range(per_head)],
                              axis=0)
        o_ref[:, hh * HEAD_DIM:(hh + 1) * HEAD_DIM] = o_h.astype(o_ref.dtype)


def _mla_prompt_attention(qmla, kmla, vmla, *, batch, seq):
    tq, tk, chain = TQ_MLA, TK_ATTN, CHAIN_MLA
    nq = seq // tq
    qk = 2 * LANES
    vmem = (2 * (tq * MLA_HEADS * qk * 2 + seq * MLA_HEADS * qk * 2 + seq * MLA_HEADS * HEAD_DIM * 2
                 + tq * MLA_HEADS * HEAD_DIM * 2) + 3 * MLA_HEADS * tq * LANES * 4 + 16 * chain * tk * 4)
    return pl.pallas_call(
        functools.partial(_mla_prompt_kernel, tq=tq, tk=tk, chain=chain),
        out_shape=jax.ShapeDtypeStruct((batch * seq, MLA_HEADS * HEAD_DIM), BF16),
        grid=(batch, nq),
        in_specs=[pl.BlockSpec((tq, MLA_HEADS * qk), lambda b, i: (b * nq + i, 0)),
                  pl.BlockSpec((seq, MLA_HEADS * qk), lambda b, i: (b, 0)),
                  pl.BlockSpec((seq, MLA_HEADS * HEAD_DIM), lambda b, i: (b, 0))],
        out_specs=pl.BlockSpec((tq, MLA_HEADS * HEAD_DIM), lambda b, i: (b * nq + i, 0)),
        compiler_params=_cparams(("arbitrary", "arbitrary"), vmem),
        name="mla_attn_prompt",
    )(qmla, kmla, vmla)


def _decode_kernel(pt_ref, lam_ref, gsub_ref, slope_ref, qlat_ref, qpe_ref, ckvn_ref, kpen_ref, qdm_ref,
                   kdn_ref, vdn_ref, c_ckv, c_kpe, c_dk, c_dv, olat_ref, od_ref,
                   ckv_buf, kpe_buf, dk_buf, dv_buf, sem, *, n_seq, n_chunks, pages, lam_init):
    s_idx = pl.program_id(0)
    chunk_len = pages * PAGE_SIZE

    def copies(seq, chunk, slot):
        out = []
        for p in range(pages):
            page = pt_ref[seq, chunk * pages + p]
            out.append(pltpu.make_async_copy(
                c_ckv.at[page], ckv_buf.at[slot, pl.ds(p * PAGE_SIZE, PAGE_SIZE), :], sem.at[slot]))
            out.append(pltpu.make_async_copy(
                c_kpe.at[page], kpe_buf.at[slot, :, pl.ds(p * PAGE_SIZE, PAGE_SIZE)], sem.at[slot]))
            out.append(pltpu.make_async_copy(
                c_dk.at[page], dk_buf.at[slot, :, pl.ds(p * PAGE_SIZE, PAGE_SIZE)], sem.at[slot]))
            out.append(pltpu.make_async_copy(
                c_dv.at[page], dv_buf.at[slot, pl.ds(2 * p * PAGE_SIZE, 2 * PAGE_SIZE), :], sem.at[slot]))
        return out

    def start_chunk(seq, chunk, slot):
        for cp in copies(seq, chunk, slot):
            cp.start()

    def wait_chunk(seq, chunk, slot):
        for cp in copies(seq, chunk, slot):
            cp.wait()

    @pl.when(s_idx == 0)
    def _():
        start_chunk(0, 0, 0)

    qlat = qlat_ref[...]
    qpe = qpe_ref[...]
    qdm = qdm_ref[...]
    qlat_bf, qpe_bf, qdm_bf = qlat.astype(BF16), qpe.astype(BF16), qdm.astype(BF16)
    ckv_new = ckvn_ref[...]
    slopes = slope_ref[...][:, 0:1]
    past_len = n_chunks * chunk_len

    m_m = (jnp.sum(qlat * ckv_new, axis=-1, keepdims=True)
           + jnp.sum(qpe * kpen_ref[...], axis=-1, keepdims=True))
    l_m = jnp.ones_like(m_m)
    acc_m = jnp.broadcast_to(ckv_new, (MLA_HEADS, KV_LORA)).astype(F32)
    m_d = jnp.sum(qdm * kdn_ref[...], axis=-1, keepdims=True)
    l_d = jnp.ones_like(m_d)
    vdn = vdn_ref[...]
    half = 2 * DIFF_GROUP
    acc_d = jnp.concatenate([jnp.broadcast_to(vdn[:, :HEAD_DIM], (half, HEAD_DIM)),
                             jnp.broadcast_to(vdn[:, HEAD_DIM:], (half, HEAD_DIM))], axis=0).astype(F32)

    def body(c, carry):
        m_m, l_m, acc_m, m_d, l_d, acc_d = carry
        step = s_idx * n_chunks + c
        slot = step % 2

        @pl.when(c + 1 < n_chunks)
        def _():
            start_chunk(s_idx, c + 1, 1 - slot)

        @pl.when(jnp.logical_and(c + 1 == n_chunks, s_idx + 1 < n_seq))
        def _():
            start_chunk(s_idx + 1, 0, 1 - slot)

        wait_chunk(s_idx, c, slot)

        ckv = ckv_buf[slot].astype(BF16)
        s_m = _dot_nt(qlat_bf, ckv) + _dot(qpe_bf, kpe_buf[slot].astype(BF16))
        m_m, l_m, acc_m = _softmax_step(s_m, ckv, m_m, l_m, acc_m)

        s_d = _dot(qdm_bf, dk_buf[slot].astype(BF16))
        pos = c * chunk_len + _lane_iota((1, chunk_len))
        s_d = s_d - slopes * (past_len - pos).astype(F32)
        m_new = jnp.maximum(m_d, jnp.max(s_d, axis=-1, keepdims=True))
        alpha = jnp.exp(m_d - m_new)
        p = jnp.exp(s_d - m_new)
        l_d = alpha * l_d + jnp.sum(p, axis=-1, keepdims=True)
        p_bf = p.astype(BF16)
        v0 = dv_buf[slot, pl.ds(0, chunk_len, stride=2), :].astype(BF16)
        v1 = dv_buf[slot, pl.ds(1, chunk_len, stride=2), :].astype(BF16)
        pv = jnp.concatenate([_dot(p_bf[:half], v0), _dot(p_bf[half:], v1)], axis=0)
        acc_d = alpha * acc_d + pv
        return m_m, l_m, acc_m, m_new, l_d, acc_d

    m_m, l_m, acc_m, m_d, l_d, acc_d = lax.fori_loop(0, n_chunks, body, (m_m, l_m, acc_m, m_d, l_d, acc_d))

    olat_ref[...] = acc_m / l_m
    o = acc_d / l_d
    lam = _diff_lambda(lam_ref[...], lam_init)
    g4 = DIFF_GROUP
    od = jnp.concatenate([o[0:g4] - lam * o[g4:2 * g4], o[2 * g4:3 * g4] - lam * o[3 * g4:4 * g4]], axis=0)
    od = od * lax.rsqrt(jnp.mean(od * od, axis=-1, keepdims=True) + RMS_EPS) * (gsub_ref[...] * (1.0 - lam_init))
    od_ref[...] = od


def _decode_attention(page_table, lam_v, gsub, slopes, qlat, qpe, ckv_new, kpe_new, qdm, kd_new, vd_new,
                      c_ckv, c_kpe, c_dk, c_dv, *, lam_init):
    n_seq, n_pages = page_table.shape
    pages = min(PAGES_PER_CHUNK, n_pages)
    n_chunks = n_pages // pages
    chunk_len = pages * PAGE_SIZE

    def per_seq(a):
        return pl.BlockSpec((None,) + a.shape[1:], lambda s, pt: (s,) + (0,) * (a.ndim - 1))

    def full(a):
        return pl.BlockSpec(a.shape, lambda s, pt: (0,) * a.ndim)

    any_spec = pl.BlockSpec(memory_space=pl.ANY)
    buf_bytes = 2 * chunk_len * (KV_LORA + MLA_ROPE + DIFF_K_COLS + DIFF_V_COLS) * 4
    grid_spec = pltpu.PrefetchScalarGridSpec(
        num_scalar_prefetch=1,
        grid=(n_seq,),
        in_specs=[full(lam_v), full(gsub), full(slopes), per_seq(qlat), per_seq(qpe), per_seq(ckv_new),
                  per_seq(kpe_new), per_seq(qdm), per_seq(kd_new), per_seq(vd_new),
                  any_spec, any_spec, any_spec, any_spec],
        out_specs=[pl.BlockSpec((None, MLA_HEADS, KV_LORA), lambda s, pt: (s, 0, 0)),
                   pl.BlockSpec((None, DIFF_HEADS, HEAD_DIM), lambda s, pt: (s, 0, 0))],
        scratch_shapes=[pltpu.VMEM((2, chunk_len, KV_LORA), F32),
                        pltpu.VMEM((2, MLA_ROPE, chunk_len), F32),
                        pltpu.VMEM((2, DIFF_K_COLS, chunk_len), F32),
                        pltpu.VMEM((2, 2 * chunk_len, HEAD_DIM), F32),
                        pltpu.SemaphoreType.DMA((2,))],
    )
    return pl.pallas_call(
        functools.partial(_decode_kernel, n_seq=n_seq, n_chunks=n_chunks, pages=pages, lam_init=lam_init),
        out_shape=[jax.ShapeDtypeStruct((n_seq, MLA_HEADS, KV_LORA), F32),
                   jax.ShapeDtypeStruct((n_seq, DIFF_HEADS, HEAD_DIM), F32)],
        grid_spec=grid_spec,
        compiler_params=_cparams(("arbitrary",), buf_bytes + buf_bytes // 2),
        name="decode_attn",
    )(page_table, lam_v, gsub, slopes, qlat, qpe, ckv_new, kpe_new, qdm, kd_new, vd_new, c_ckv, c_kpe, c_dk, c_dv)


def _latent_out_kernel(o_ref, w_ref, out_ref):
    out_ref[...] = _dot(o_ref[...].astype(BF16), w_ref[...]).astype(out_ref.dtype)


def _latent_out(olat2d, w_uv_bf):
    n = olat2d.shape[0]
    return pl.pallas_call(
        _latent_out_kernel,
        out_shape=jax.ShapeDtypeStruct((n, MLA_HEADS * HEAD_DIM), BF16),
        grid=(MLA_HEADS,),
        in_specs=[pl.BlockSpec((n, KV_LORA), lambda h: (0, h)),
                  pl.BlockSpec((None, KV_LORA, HEAD_DIM), lambda h: (h, 0, 0))],
        out_specs=pl.BlockSpec((n, HEAD_DIM), lambda h: (0, h)),
        compiler_params=_cparams(("arbitrary",), 4 * n * KV_LORA * 4),
        name="latent_out",
    )(olat2d, w_uv_bf)


def _pack_bf16_pairs(a, b):
    ua = pltpu.bitcast(a.astype(BF16).astype(F32), U32) & jnp.uint32(0xFFFF0000)
    ub = pltpu.bitcast(b.astype(BF16).astype(F32), U32) >> jnp.uint32(16)
    return ua | ub


def _unpack_bf16_pairs(u):
    a = pltpu.bitcast(u & jnp.uint32(0xFFFF0000), F32)
    b = pltpu.bitcast(u << jnp.uint32(16), F32)
    return a, b


def _outproj_kernel(ad_ref, am_ref, x_ref, gt_ref, sh_ref, sc_ref, g2_ref, wo_ref, wr_ref, br_ref, tri_ref, cin_ref,
                    x1_ref, h2_ref, slab_ref, cout_ref, carry_ref):
    i = pl.program_id(0)

    @pl.when(i == 0)
    def _():
        carry_ref[...] = cin_ref[...]

    half = wo_ref.shape[0] // 2
    mix = _dot(ad_ref[...], wo_ref[:half, :]) + _dot(am_ref[...], wo_ref[half:, :])
    x1 = x_ref[...] + gt_ref[...] * mix
    x1_ref[...] = x1
    h = x1 * lax.rsqrt(jnp.mean(x1 * x1, axis=-1, keepdims=True) + RMS_EPS) * g2_ref[...]
    h = h * (1.0 + sc_ref[...]) + sh_ref[...]
    d = h.shape[1]
    h2_ref[...] = _pack_bf16_pairs(h[:, :d // 2], h[:, d // 2:])

    h_hi = h.astype(BF16)
    h_lo = (h - h_hi.astype(F32)).astype(BF16)
    r2 = _dot(jnp.concatenate([h_hi, h_lo], axis=1), wr_ref[...])
    ne = r2.shape[1] // 2
    logits = r2[:, :ne] + r2[:, ne:] + br_ref[...]
    tm = logits.shape[0]
    lane = _lane_iota((tm, ne)).astype(F32)
    slab_lane = _lane_iota((tm, LANES))
    carry = carry_ref[...]
    work = logits
    onehots, vals, idxs = [], [], []
    for _ in range(TOP_K):
        mx = jnp.max(work, axis=-1, keepdims=True)
        ix = jnp.min(jnp.where(work == mx, lane, float(ne)), axis=-1, keepdims=True)
        oh = lane == ix
        onehots.append(oh)
        vals.append(mx)
        idxs.append(ix)
        work = jnp.where(oh, -jnp.inf, work)
    cnt = sum(jnp.where(oh, 1.0, 0.0) for oh in onehots)
    before = _dot(tri_ref[...], cnt.astype(BF16)) + carry
    exps = [jnp.exp(v - vals[0]) for v in vals]
    den = sum(exps)
    slab = jnp.zeros((tm, LANES), F32)
    for k in range(TOP_K):
        rank = jnp.sum(jnp.where(onehots[k], before, 0.0), axis=-1, keepdims=True)
        slab = jnp.where(slab_lane == k, idxs[k], slab)
        slab = jnp.where(slab_lane == TOP_K + k, rank, slab)
        slab = jnp.where(slab_lane == 2 * TOP_K + k, exps[k] / den, slab)
    slab_ref[...] = slab
    new_carry = carry + jnp.sum(cnt, axis=0, keepdims=True)
    carry_ref[...] = new_carry
    cout_ref[...] = new_carry


def _out_project(attn_d, attn_m, x2d, gate, shift, scale, consts, count_in, *, rows_per_mod, tm, name):
    n, d = x2d.shape
    steps = n // tm
    mod_rows = gate.shape[1]
    mod_spec = pl.BlockSpec((None, mod_rows, d), lambda i: (i * tm // rows_per_mod, 0, 0))

    def full(a):
        nd = a.ndim
        return pl.BlockSpec(a.shape, lambda i: (0,) * nd)

    in_specs = [pl.BlockSpec((tm, d // 2), lambda i: (i, 0)), pl.BlockSpec((tm, d // 2), lambda i: (i, 0)),
                pl.BlockSpec((tm, d), lambda i: (i, 0)), mod_spec, mod_spec, mod_spec] + [full(a) for a in consts] + [full(count_in)]
    out_shape = [jax.ShapeDtypeStruct((n, d), F32), jax.ShapeDtypeStruct((n, d // 2), U32),
                 jax.ShapeDtypeStruct((n, LANES), F32), jax.ShapeDtypeStruct(count_in.shape, F32)]
    out_specs = [pl.BlockSpec((tm, d), lambda i: (i, 0)), pl.BlockSpec((tm, d // 2), lambda i: (i, 0)),
                 pl.BlockSpec((tm, LANES), lambda i: (i, 0)), full(count_in)]
    weight_bytes = sum(int(a.size) * a.dtype.itemsize for a in consts)
    return pl.pallas_call(
        _outproj_kernel, out_shape=out_shape, grid=(steps,), in_specs=in_specs, out_specs=out_specs,
        scratch_shapes=[pltpu.VMEM(count_in.shape, F32)],
        compiler_params=_cparams(("arbitrary",), 2 * weight_bytes + 16 * tm * d * 4),
        name=name,
    )(attn_d, attn_m, x2d, gate, shift, scale, *consts, count_in)


ZERO_CHUNK = 128


def _zero_row_ranges(lo_ref, hi_ref, zbuf, xs_out, sem, n_ranges):
    def pieces(g):
        lo, hi = lo_ref[g], hi_ref[g]
        n1 = (-lo) & 7
        a8 = lo + n1
        n8 = ((-a8) & (ZERO_CHUNK - 1)) >> 3
        a_big = a8 + 8 * n8
        n_big = (hi - a_big) // ZERO_CHUNK
        return ((n1, 1, lambda t: lo + t),
                (n8, 8, lambda t: pl.multiple_of(a8 + 8 * t, 8)),
                (n_big, ZERO_CHUNK, lambda t: pl.multiple_of(a_big + ZERO_CHUNK * t, ZERO_CHUNK)))

    def for_all(action):
        def per_range(g, c):
            for count, rows, row_of in pieces(g):
                def one(t, cc, rows=rows, row_of=row_of):
                    action(pltpu.make_async_copy(zbuf.at[pl.ds(0, rows), :], xs_out.at[pl.ds(row_of(t), rows), :], sem))
                    return cc
                lax.fori_loop(0, count, one, 0)
            return c
        lax.fori_loop(0, n_ranges, per_range, 0)

    for_all(lambda cp: cp.start())
    for_all(lambda cp: cp.wait())


def _dispatch_kernel(zlo_ref, zhi_ref, dest_ref, hp_ref, hs_ref, xs_out, zbuf, sem, *, tm, prompt_steps, n_ranges):
    @pl.when(pl.program_id(0) == 0)
    def _():
        zbuf[...] = jnp.zeros(zbuf.shape, zbuf.dtype)
        _zero_row_ranges(zlo_ref, zhi_ref, zbuf, xs_out, sem, n_ranges)

    def scatter_rows(h_ref):
        def row_copy(r, k):
            return pltpu.make_async_copy(h_ref.at[pl.ds(r, 1), :],
                                         xs_out.at[pl.ds(dest_ref[r * TOP_K + k], 1), :], sem)

        def issue(r, c):
            for k in range(TOP_K):
                row_copy(r, k).start()
            return c

        def drain(r, c):
            for k in range(TOP_K):
                row_copy(r, k).wait()
            return c

        lax.fori_loop(0, tm, issue, 0, unroll=DMA_UNROLL)
        lax.fori_loop(0, tm, drain, 0, unroll=DMA_UNROLL)

    @pl.when(pl.program_id(0) < prompt_steps)
    def _():
        scatter_rows(hp_ref)

    @pl.when(pl.program_id(0) >= prompt_steps)
    def _():
        scatter_rows(hs_ref)


def _dispatch(h2_p, h2_s, dest_flat, zero_lo, zero_hi, n_rows):
    tm = TM_TOK
    w = h2_p.shape[1]
    p_steps, s_steps = h2_p.shape[0] // tm, h2_s.shape[0] // tm
    grid_spec = pltpu.PrefetchScalarGridSpec(
        num_scalar_prefetch=2,
        grid=(p_steps + s_steps,),
        in_specs=[pl.BlockSpec((tm * TOP_K,), lambda i, zl, zh: (i,), memory_space=pltpu.SMEM),
                  pl.BlockSpec((tm, w), lambda i, zl, zh: (jnp.minimum(i, p_steps - 1), 0)),
                  pl.BlockSpec((tm, w), lambda i, zl, zh: (jnp.maximum(i - p_steps, 0), 0))],
        out_specs=pl.BlockSpec(memory_space=pl.ANY),
        scratch_shapes=[pltpu.VMEM((ZERO_CHUNK, w), U32), pltpu.SemaphoreType.DMA(())])
    return pl.pallas_call(
        functools.partial(_dispatch_kernel, tm=tm, prompt_steps=p_steps, n_ranges=zero_lo.shape[0]),
        out_shape=jax.ShapeDtypeStruct((n_rows, w), U32),
        grid_spec=grid_spec,
        compiler_params=pltpu.CompilerParams(dimension_semantics=("arbitrary",), has_side_effects=True),
        name="moe_dispatch",
    )(zero_lo, zero_hi, dest_flat, h2_p, h2_s)


def _expert_changed(be_ref, i):
    prev = be_ref[jnp.maximum(i - 1, 0)]
    return jnp.logical_or(i == 0, be_ref[i] != prev)


def _gate_up_kernel(be_ref, nu_ref, xs_ref, wg_ref, wu_ref, bg_ref, bu_ref, act_ref, wg_bf, wu_bf):
    i = pl.program_id(1)

    @pl.when(jnp.logical_and(i < nu_ref[0], _expert_changed(be_ref, i)))
    def _():
        wg_bf[...] = wg_ref[...].astype(BF16)
        wu_bf[...] = wu_ref[...].astype(BF16)

    @pl.when(i < nu_ref[0])
    def _():
        a, b = _unpack_bf16_pairs(xs_ref[...])
        x = jnp.concatenate([a, b], axis=1).astype(BF16)
        gate = jnp.minimum(_dot(x, wg_bf[...]) + bg_ref[...], SWIGLU_LIMIT)
        up = jnp.clip(_dot(x, wu_bf[...]) + bu_ref[...], -SWIGLU_LIMIT, SWIGLU_LIMIT)
        act_ref[...] = ((up + 1.0) * gate * jax.nn.sigmoid(SWIGLU_ALPHA * gate)).astype(act_ref.dtype)

    @pl.when(i >= nu_ref[0])
    def _():
        act_ref[...] = jnp.zeros_like(act_ref)


def _down_kernel(be_ref, nu_ref, act_ref, wa_ref, wb_ref, ba_ref, bb_ref, y_ref, wa_bf, wb_bf):
    i = pl.program_id(1)

    @pl.when(jnp.logical_and(i < nu_ref[0], _expert_changed(be_ref, i)))
    def _():
        wa_bf[...] = wa_ref[...].astype(BF16)
        wb_bf[...] = wb_ref[...].astype(BF16)

    @pl.when(i < nu_ref[0])
    def _():
        act = act_ref[...]
        ya = _dot(act, wa_bf[...]) + ba_ref[...]
        yb = _dot(act, wb_bf[...]) + bb_ref[...]
        y_ref[...] = _pack_bf16_pairs(ya, yb)

    @pl.when(i >= nu_ref[0])
    def _():
        y_ref[...] = jnp.zeros_like(y_ref)


def _experts(xs, block_exp, n_used, w_gu, b_gu, w_down, b_down):
    n_rows, half_d = xs.shape
    d = 2 * half_d
    ne, _, two_f = w_gu.shape
    f = two_f // 2
    tb, tn, tn2 = TB_MOE, TN_GU, TN_DOWN
    n_blk = n_rows // tb
    b_gu3 = b_gu.reshape(ne, 1, two_f)
    b_down3 = b_down.reshape(ne, 1, d)

    def blk(i, nu):
        return jnp.minimum(i, nu[0] - 1)

    up_off = f // tn
    gs1 = pltpu.PrefetchScalarGridSpec(
        num_scalar_prefetch=2, grid=(f // tn, n_blk),
        in_specs=[pl.BlockSpec((tb, half_d), lambda j, i, be, nu: (blk(i, nu), 0)),
                  pl.BlockSpec((None, d, tn), lambda j, i, be, nu: (be[blk(i, nu)], 0, j)),
                  pl.BlockSpec((None, d, tn), lambda j, i, be, nu: (be[blk(i, nu)], 0, j + up_off)),
                  pl.BlockSpec((None, 1, tn), lambda j, i, be, nu: (be[blk(i, nu)], 0, j)),
                  pl.BlockSpec((None, 1, tn), lambda j, i, be, nu: (be[blk(i, nu)], 0, j + up_off))],
        out_specs=pl.BlockSpec((tb, tn), lambda j, i, be, nu: (i, j)),
        scratch_shapes=[pltpu.VMEM((d, tn), BF16), pltpu.VMEM((d, tn), BF16)])
    act = pl.pallas_call(
        _gate_up_kernel, out_shape=jax.ShapeDtypeStruct((n_rows, f), BF16), grid_spec=gs1,
        compiler_params=_cparams(("arbitrary", "arbitrary"), 4 * d * tn * 4 + 2 * d * tn * 2 + 8 * tb * d * 4),
        name="moe_gate_up",
    )(block_exp, n_used, xs, w_gu, w_gu, b_gu3, b_gu3)

    hi_off = half_d // tn2
    gs2 = pltpu.PrefetchScalarGridSpec(
        num_scalar_prefetch=2, grid=(half_d // tn2, n_blk),
        in_specs=[pl.BlockSpec((tb, f), lambda j, i, be, nu: (blk(i, nu), 0)),
                  pl.BlockSpec((None, f, tn2), lambda j, i, be, nu: (be[blk(i, nu)], 0, j)),
                  pl.BlockSpec((None, f, tn2), lambda j, i, be, nu: (be[blk(i, nu)], 0, j + hi_off)),
                  pl.BlockSpec((None, 1, tn2), lambda j, i, be, nu: (be[blk(i, nu)], 0, j)),
                  pl.BlockSpec((None, 1, tn2), lambda j, i, be, nu: (be[blk(i, nu)], 0, j + hi_off))],
        out_specs=pl.BlockSpec((tb, tn2), lambda j, i, be, nu: (i, j)),
        scratch_shapes=[pltpu.VMEM((f, tn2), BF16), pltpu.VMEM((f, tn2), BF16)])
    return pl.pallas_call(
        _down_kernel, out_shape=jax.ShapeDtypeStruct((n_rows, half_d), U32), grid_spec=gs2,
        compiler_params=_cparams(("arbitrary", "arbitrary"), 4 * f * tn2 * 4 + 2 * f * tn2 * 2 + 8 * tb * f * 2),
        name="moe_down",
    )(block_exp, n_used, act, w_down, w_down, b_down3, b_down3)


def _combine_kernel(dest_ref, x1_ref, gt_ref, slab_ref, yb_ref, y_ref, buf, sem, *, tm):
    def row_copy(r, k):
        return pltpu.make_async_copy(yb_ref.at[pl.ds(dest_ref[r * TOP_K + k], 1), :],
                                     buf.at[k, pl.ds(r, 1), :], sem)

    def issue(r, c):
        for k in range(TOP_K):
            row_copy(r, k).start()
        return c

    def drain(r, c):
        for k in range(TOP_K):
            row_copy(r, k).wait()
        return c

    lax.fori_loop(0, tm, issue, 0, unroll=DMA_UNROLL)
    lax.fori_loop(0, tm, drain, 0, unroll=DMA_UNROLL)
    slab = slab_ref[...]
    acc_a = None
    for k in range(TOP_K):
        a, b = _unpack_bf16_pairs(buf[k])
        g = slab[:, 2 * TOP_K + k:2 * TOP_K + k + 1]
        acc_a = g * a if acc_a is None else acc_a + g * a
        acc_b = g * b if k == 0 else acc_b + g * b
    moe = jnp.concatenate([acc_a, acc_b], axis=1)
    y_ref[...] = x1_ref[...] + gt_ref[...] * moe


def _combine(x1, gate, slab, dest_flat, yb, *, rows_per_mod, tm):
    n, d = x1.shape
    mod_rows = gate.shape[1]
    return pl.pallas_call(
        functools.partial(_combine_kernel, tm=tm),
        out_shape=jax.ShapeDtypeStruct((n, d), F32),
        grid=(n // tm,),
        in_specs=[pl.BlockSpec((tm * TOP_K,), lambda i: (i,), memory_space=pltpu.SMEM),
                  pl.BlockSpec((tm, d), lambda i: (i, 0)),
                  pl.BlockSpec((None, mod_rows, d), lambda i: (i * tm // rows_per_mod, 0, 0)),
                  pl.BlockSpec((tm, LANES), lambda i: (i, 0)),
                  pl.BlockSpec(memory_space=pl.ANY)],
        out_specs=pl.BlockSpec((tm, d), lambda i: (i, 0)),
        scratch_shapes=[pltpu.VMEM((TOP_K, tm, d // 2), U32), pltpu.SemaphoreType.DMA(())],
        compiler_params=_cparams(("arbitrary",), 8 * tm * d * 4),
        name="moe_combine",
    )(dest_flat, x1, gate, slab, yb)


def _rope_table(pos):
    inv = ROPE_THETA ** (-jnp.arange(0, MLA_ROPE, 2, dtype=F32) / MLA_ROPE)
    ang = pos.astype(F32)[:, None] * inv[None, :]
    c, s = jnp.cos(ang), jnp.sin(ang)
    return jnp.concatenate([c, c, -s, s], axis=-1)


def _swap_halves(a):
    h = a.shape[-1] // 2
    return jnp.concatenate([a[..., h:], a[..., :h]], axis=-1)


def kernel(x_prompt, x_sample, cache_dk, cache_dv, cache_ckv, cache_kpe, page_table, c_prompt, c_sample, norm1_g, norm2_g, w_ada, b_ada, w_in, diff_qn_g, diff_kn_g, diff_lambda, diff_subln_g, mla_qa_g, w_qb, mla_qn_g, mla_kva_g, mla_kpe_g, w_uk, w_uv, w_out, w_router, b_router, w_gu, b_gu, w_down, b_down):
    depth = w_in.shape[0]
    assert depth == 1, "single-layer trunk"
    batch, seq, d = x_prompt.shape
    n_seq, dec_seq, _ = x_sample.shape
    assert dec_seq == 1
    n_pool = cache_ckv.shape[1]
    n_pages = page_table.shape[1]
    past_len = n_pages * PAGE_SIZE
    n_p = batch * seq
    n_all = n_p + n_seq
    lam_init = 0.8 - 0.6 * math.exp(-0.3 * 0)
    assert seq % TM_PROJ == 0 and seq % TQ_MLA == 0 and seq % TK_ATTN == 0 and n_p % TM_TOK == 0 and n_seq % TM_TOK == 0
    assert TB_MOE % ZERO_CHUNK == 0

    w_in0 = w_in[0]
    w_in_ext = jnp.concatenate([w_in0, _swap_halves(w_in0[:, O_KPE:])], axis=1).astype(BF16)
    gmat = (jnp.arange(LANES)[:, None] // DIFF_DH == jnp.arange(LANES)[None, :] // DIFF_DH).astype(BF16)
    gdq = (jnp.tile(diff_qn_g[0].reshape(-1), DIFF_HEADS) * (DIFF_DH ** -0.5)).reshape(1, -1)
    gdk = jnp.tile(diff_kn_g[0].reshape(-1), DIFF_KV_HEADS).reshape(1, -1)
    gkpe = jnp.concatenate([mla_kpe_g[0], _swap_halves(mla_kpe_g[0])]).reshape(1, -1)
    qscale = MLA_QK_DIM ** -0.5
    gqn = (mla_qn_g[0, :MLA_NOPE] * qscale).reshape(1, -1)
    gq_pe = mla_qn_g[0, MLA_NOPE:]
    gqp = (jnp.concatenate([gq_pe, _swap_halves(gq_pe)]) * qscale).reshape(1, -1)
    wqb3 = w_qb[0].reshape(Q_LORA, MLA_HEADS, MLA_QK_DIM)
    wqb_pe = wqb3[:, :, MLA_NOPE:]
    wqb_ext = jnp.concatenate([wqb3[:, :, :MLA_NOPE].reshape(Q_LORA, -1),
                               jnp.concatenate([wqb_pe, _swap_halves(wqb_pe)], axis=-1).reshape(Q_LORA, -1)],
                              axis=1).astype(BF16)
    w_kvup = jnp.concatenate([jnp.transpose(w_uk[0], (1, 0, 2)).reshape(KV_LORA, -1),
                              jnp.transpose(w_uv[0], (1, 0, 2)).reshape(KV_LORA, -1)], axis=1).astype(BF16)
    w_uk_t = jnp.transpose(w_uk[0], (0, 2, 1)).astype(BF16)
    w_uv_bf = w_uv[0].astype(BF16)
    proj_consts = [norm1_g[0].reshape(1, -1), w_in_ext, gmat, gdq, gdk, mla_qa_g[0].reshape(1, -1),
                   mla_kva_g[0].reshape(1, -1), gkpe, wqb_ext, gqn, gqp]
    lam_v = diff_lambda[0]
    gsub = diff_subln_g[0].reshape(1, -1)

    mod = _modulation(jnp.concatenate([c_prompt, c_sample], axis=0), w_ada[0], b_ada[0])
    mod_p = mod[:batch].reshape(batch, 6, 1, d)
    mod_s = mod[batch:].reshape(n_seq, 6, d)
    sh1_p, sc1_p, gt1_p, sh2_p, sc2_p, gt2_p = [mod_p[:, t] for t in range(6)]
    sh1_s, sc1_s, gt1_s, sh2_s, sc2_s, gt2_s = [mod_s[:, t][None] for t in range(6)]

    xp2 = x_prompt.reshape(n_p, d)
    xs2 = x_sample.reshape(n_seq, d)
    tab_p = _rope_table(jnp.arange(seq, dtype=I32))
    tab_s = _rope_table(jnp.full((1,), past_len, I32))
    (qds, dk_p, dv_p, ckv_p, kpe_p, qmla, kmla, vmla) = _project(
        xp2, sh1_p, sc1_p, tab_p, proj_consts, w_kvup, sample=False, rows_per_mod=seq, tm=TM_PROJ)
    (qd_s, dk_s, dv_s, ckv_s, kpe_s, qlat_s, qpe_s) = _project(
        xs2, sh1_s, sc1_s, tab_s, proj_consts, w_uk_t, sample=True, rows_per_mod=n_seq, tm=n_seq)

    od_p = _diff_prompt_attention(qds, dk_p, dv_p, lam_v, gsub, batch=batch, seq=seq, lam_init=lam_init)
    om_p = _mla_prompt_attention(qmla, kmla, vmla, batch=batch, seq=seq)

    c_ckv = cache_ckv.reshape(n_pool, PAGE_SIZE, KV_LORA)
    c_kpe = jnp.transpose(cache_kpe, (0, 1, 3, 2)).reshape(n_pool, MLA_ROPE, PAGE_SIZE)
    c_dk = jnp.transpose(cache_dk, (0, 1, 3, 4, 5, 2)).reshape(n_pool, DIFF_K_COLS, PAGE_SIZE)
    c_dv = cache_dv.reshape(n_pool, PAGE_SIZE * DIFF_KV_HEADS, HEAD_DIM)
    q5 = qd_s.reshape(n_seq, DIFF_KV_HEADS, DIFF_GROUP, 2, DIFF_DH)
    eye_kv = jnp.eye(DIFF_KV_HEADS, dtype=F32)
    eye_m = jnp.eye(2, dtype=F32)
    qdm = jnp.einsum("skgmd,kK,mM->skmgKMd", q5, eye_kv, eye_m).reshape(n_seq, 2 * DIFF_HEADS, DIFF_K_COLS)
    head_id = jnp.arange(2 * DIFF_HEADS) // (2 * DIFF_GROUP) * DIFF_GROUP + jnp.arange(2 * DIFF_HEADS) % DIFF_GROUP
    slopes = jnp.broadcast_to((2.0 ** (-(head_id + 1).astype(F32)))[:, None], (2 * DIFF_HEADS, LANES))
    qpe3 = qpe_s.reshape(n_seq, MLA_HEADS, LANES)[:, :, :MLA_ROPE]
    olat_s, od_s = _decode_attention(
        page_table, lam_v, gsub, slopes, qlat_s.reshape(n_seq, MLA_HEADS, KV_LORA), qpe3,
        ckv_s.reshape(n_seq, 1, KV_LORA), kpe_s.reshape(n_seq, 1, MLA_ROPE), qdm,
        dk_s.reshape(n_seq, 1, DIFF_K_COLS), dv_s.reshape(n_seq, 1, DIFF_V_COLS),
        c_ckv, c_kpe, c_dk, c_dv, lam_init=lam_init)
    om_s = _latent_out(olat_s.reshape(n_seq, MLA_HEADS * KV_LORA), w_uv_bf)
    od_s = od_s.reshape(n_seq, DIFF_HEADS * HEAD_DIM).astype(BF16)

    tri = (jnp.arange(TM_PROJ)[:, None] > jnp.arange(TM_PROJ)[None, :]).astype(BF16)
    wr_hi = w_router[0].astype(BF16)
    wr_lo = (w_router[0] - wr_hi.astype(F32)).astype(BF16)
    wr_split = jnp.concatenate([jnp.concatenate([wr_hi, wr_lo], axis=1),
                                jnp.concatenate([wr_hi, jnp.zeros_like(wr_lo)], axis=1)], axis=0)
    out_consts = [norm2_g[0].reshape(1, -1), w_out[0].astype(BF16), wr_split, b_router[0].reshape(1, -1)]
    zero_counts = jnp.zeros((1, N_EXPERTS), F32)
    x1_p, h2_p, slab_p, counts_p = _out_project(
        od_p, om_p, xp2, gt1_p, sh2_p, sc2_p, out_consts + [tri], zero_counts,
        rows_per_mod=seq, tm=TM_PROJ, name="out_proj_router_prompt")
    x1_s, h2_s, slab_s, counts = _out_project(
        od_s, om_s, xs2, gt1_s, sh2_s, sc2_s, out_consts + [tri[:n_seq, :n_seq]], counts_p,
        rows_per_mod=n_seq, tm=n_seq, name="out_proj_router_sample")

    tb = TB_MOE
    n_blk = (n_all * TOP_K + N_EXPERTS * (tb - 1)) // tb + 1
    cnt = counts[0].astype(I32)
    padded = (cnt + tb - 1) // tb * tb
    pad_end = jnp.cumsum(padded)
    pad_start = pad_end - padded
    slab = jnp.concatenate([slab_p, slab_s], axis=0)
    e_idx = slab[:, :TOP_K].astype(I32)
    rank = slab[:, TOP_K:2 * TOP_K].astype(I32)
    experts = jnp.arange(N_EXPERTS, dtype=I32)
    start_of = jnp.sum(jnp.where(e_idx[..., None] == experts, pad_start, 0), axis=-1)
    dest = (start_of + rank).reshape(-1)
    n_used = (pad_end[-1] // tb).astype(I32).reshape(1)
    blk_start = jnp.arange(n_blk, dtype=I32) * tb
    block_exp = jnp.minimum(jnp.sum((pad_end[None, :] <= blk_start[:, None]).astype(I32), axis=1), N_EXPERTS - 1)

    n_rows = n_blk * tb
    zero_lo = jnp.concatenate([pad_start + cnt, pad_end[-1:]]).astype(I32)
    zero_hi = jnp.concatenate([pad_end, jnp.full((1,), n_rows, I32)]).astype(I32)
    xs = _dispatch(h2_p, h2_s, dest, zero_lo, zero_hi, n_rows)
    yb = _experts(xs, block_exp, n_used, w_gu[0], b_gu[0], w_down[0], b_down[0])
    y_p = _combine(x1_p, gt2_p, slab_p, dest[:n_p * TOP_K], yb, rows_per_mod=seq, tm=TM_TOK)
    y_s = _combine(x1_s, gt2_s, slab_s, dest[n_p * TOP_K:], yb, rows_per_mod=n_seq, tm=TM_TOK)

    def kv_outputs(dk, dv, ckv, kpe, b, t):
        return (dk.reshape(1, b, t, DIFF_KV_HEADS, 2, DIFF_DH), dv.reshape(1, b, t, DIFF_KV_HEADS, HEAD_DIM),
                ckv.reshape(1, b, t, KV_LORA), kpe.reshape(1, b, t, MLA_ROPE))

    dk_p = jnp.transpose(dk_p.reshape(batch, DIFF_KV_HEADS, 2, DIFF_DH, seq), (0, 4, 1, 2, 3))[None]
    kpe_p = jnp.transpose(kpe_p, (0, 2, 1))[None]
    return ((y_p.reshape(batch, seq, d), y_s.reshape(n_seq, 1, d), dk_p, dv_p.reshape(1, batch, seq, DIFF_KV_HEADS, HEAD_DIM),
             ckv_p.reshape(1, batch, seq, KV_LORA), kpe_p) + kv_outputs(dk_s, dv_s, ckv_s, kpe_s, n_seq, 1))
```

```python
import functools
import math

import jax
import jax.numpy as jnp
from jax import lax
from jax.experimental import pallas as pl
from jax.experimental.pallas import tpu as pltpu

F32 = jnp.float32
BF16 = jnp.bfloat16
U32 = jnp.uint32
I32 = jnp.int32

LANES = 128
V7X_VMEM_BYTES = 64 * 1024 * 1024

HEAD_DIM = 128
DIFF_KV_HEADS = 2
DIFF_GROUP = 4
DIFF_HEADS = DIFF_KV_HEADS * DIFF_GROUP
DIFF_DH = 64
MLA_HEADS = 8
Q_LORA = 512
KV_LORA = 512
MLA_NOPE = 128
MLA_ROPE = 64
MLA_QK_DIM = MLA_NOPE + MLA_ROPE
ROPE_THETA = 10000.0
N_EXPERTS = 32
TOP_K = 4
SWIGLU_LIMIT = 7.0
SWIGLU_ALPHA = 1.702
PAGE_SIZE = 128
RMS_EPS = 1e-6
NEG_INF = -1e30

DIFF_Q_COLS = DIFF_HEADS * 2 * DIFF_DH
DIFF_K_COLS = DIFF_KV_HEADS * 2 * DIFF_DH
DIFF_V_COLS = DIFF_KV_HEADS * HEAD_DIM
O_DQ = 0
O_DK = O_DQ + DIFF_Q_COLS
O_DV = O_DK + DIFF_K_COLS
O_MQ = O_DV + DIFF_V_COLS
O_CKV = O_MQ + Q_LORA
O_KPE = O_CKV + KV_LORA
IN_COLS = O_KPE + MLA_ROPE
IN_COLS_EXT = IN_COLS + MLA_ROPE

TM_PROJ = 256
TQ_DIFF = 256
TQ_MLA = 256
CHAIN_MLA = 128
TK_ATTN = 256
TM_TOK = 128
DMA_UNROLL = 8
TB_MOE = 512
TN_GU = 1024
TN_DOWN = 1024
TN_ADA = 1024
PAGES_PER_CHUNK = 16


def _vmem_limit(nbytes):
    return int(min(max(nbytes * 5 // 4, 16 * 1024 * 1024), V7X_VMEM_BYTES - 8 * 1024 * 1024))


def _cparams(sem, vmem_bytes):
    return pltpu.CompilerParams(dimension_semantics=sem, vmem_limit_bytes=_vmem_limit(vmem_bytes))


def _dot(a, b):
    return jnp.dot(a, b, preferred_element_type=F32)


def _dot_nt(a, b):
    return lax.dot_general(a, b, (((1,), (1,)), ((), ())), preferred_element_type=F32)


def _lane_iota(shape):
    return lax.broadcasted_iota(I32, shape, len(shape) - 1)


def _mod_kernel(c_ref, w_ref, b_ref, o_ref):
    c = c_ref[...]
    a = (c * jax.nn.sigmoid(c)).astype(BF16)
    o_ref[...] = _dot(a, w_ref[...].astype(BF16)) + b_ref[...]


def _modulation(c_all, w_ada, b_ada):
    r, d = c_all.shape
    n = w_ada.shape[1]
    tn = TN_ADA
    return pl.pallas_call(
        _mod_kernel,
        out_shape=jax.ShapeDtypeStruct((r, n), F32),
        grid=(n // tn,),
        in_specs=[pl.BlockSpec((r, d), lambda j: (0, 0)),
                  pl.BlockSpec((d, tn), lambda j: (0, j)),
                  pl.BlockSpec((1, tn), lambda j: (0, j))],
        out_specs=pl.BlockSpec((r, tn), lambda j: (0, j)),
        compiler_params=_cparams(("arbitrary",), 2 * d * tn * 4 + d * tn * 2 + 4 * r * d * 4),
        name="adaln_modulation",
    )(c_all, w_ada, b_ada.reshape(1, n))


def _group64_sumsq(x, gmat):
    x2 = (x * x).astype(BF16)
    outs = [_dot(x2[:, j * LANES:(j + 1) * LANES], gmat) for j in range(x.shape[1] // LANES)]
    return outs[0] if len(outs) == 1 else jnp.concatenate(outs, axis=1)


def _rope_pair_block(blk, tab):
    r = blk * tab
    return r + pltpu.roll(r, 64, 1)


def _proj_kernel(x_ref, sh_ref, sc_ref, tab_ref, g1_ref, win_ref, gmat_ref, gdq_ref, gdk_ref, gqa_ref,
                 gckv_ref, gkpe_ref, wqb_ref, gqn_ref, gqp_ref, wup_ref, *out_refs, sample):
    x = x_ref[...]
    h = x * lax.rsqrt(jnp.mean(x * x, axis=-1, keepdims=True) + RMS_EPS) * g1_ref[...]
    h = h * (1.0 + sc_ref[...]) + sh_ref[...]
    z = _dot(h.astype(BF16), win_ref[...])
    gmat = gmat_ref[...]
    tab = tab_ref[...]
    lane = _lane_iota((1, LANES))
    low_half = lane < 64

    dq = z[:, O_DQ:O_DK]
    qd = dq * lax.rsqrt(_group64_sumsq(dq, gmat) * (1.0 / DIFF_DH) + RMS_EPS) * gdq_ref[...]
    dk = z[:, O_DK:O_DV]
    kd = dk * lax.rsqrt(_group64_sumsq(dk, gmat) * (1.0 / DIFF_DH) + RMS_EPS) * gdk_ref[...]
    vd = z[:, O_DV:O_MQ]
    mq = z[:, O_MQ:O_CKV]
    ckv = z[:, O_CKV:O_KPE]
    kpb = z[:, O_KPE:IN_COLS_EXT]

    ckv_n = ckv * lax.rsqrt(jnp.mean(ckv * ckv, axis=-1, keepdims=True) + RMS_EPS) * gckv_ref[...]
    kp_ms = 0.5 * jnp.sum(kpb * kpb, axis=-1, keepdims=True) * (1.0 / MLA_ROPE)
    kpn = kpb * lax.rsqrt(kp_ms + RMS_EPS) * gkpe_ref[...]
    kr = _rope_pair_block(kpn, tab)

    mqn = (mq * lax.rsqrt(jnp.mean(mq * mq, axis=-1, keepdims=True) + RMS_EPS) * gqa_ref[...]).astype(BF16)
    qm = _dot(mqn, wqb_ref[...])
    gqn = gqn_ref[...]
    gqp = gqp_ref[...]
    q_nope, q_rope = [], []
    for hh in range(MLA_HEADS):
        n_h = qm[:, hh * LANES:(hh + 1) * LANES]
        p_h = qm[:, (MLA_HEADS + hh) * LANES:(MLA_HEADS + hh + 1) * LANES]
        ms = (jnp.sum(n_h * n_h, axis=-1, keepdims=True)
              + 0.5 * jnp.sum(p_h * p_h, axis=-1, keepdims=True)) * (1.0 / MLA_QK_DIM)
        r = lax.rsqrt(ms + RMS_EPS)
        q_nope.append(n_h * r * gqn)
        q_rope.append(_rope_pair_block(p_h * r * gqp, tab))

    if sample:
        qd_ref, dk_ref, dv_ref, ckv_ref, kpe_ref, qlat_ref, qpe_ref = out_refs
        qd_ref[...] = qd
        qlat_ref[...] = jnp.concatenate(
            [_dot(q_nope[hh].astype(BF16), wup_ref[hh]) for hh in range(MLA_HEADS)], axis=1)
        qpe_ref[...] = jnp.concatenate(q_rope, axis=1)
    else:
        qds_ref, dk_ref, dv_ref, ckv_ref, kpe_ref, qmla_ref, kmla_ref, vmla_ref = out_refs
        blocks = []
        for c in range(DIFF_HEADS):
            blk = qd[:, c * LANES:(c + 1) * LANES]
            blocks.append(jnp.where(low_half, blk, 0.0))
            blocks.append(jnp.where(low_half, 0.0, blk))
        qds_ref[...] = jnp.concatenate(blocks, axis=1).astype(BF16)
        kvup = _dot(ckv_n.astype(BF16), wup_ref[...])
        kr_lo = jnp.where(low_half, kr, 0.0)
        qb, kb = [], []
        for hh in range(MLA_HEADS):
            qb += [q_nope[hh], q_rope[hh]]
            kb += [kvup[:, hh * LANES:(hh + 1) * LANES], kr_lo]
        qmla_ref[...] = jnp.concatenate(qb, axis=1).astype(BF16)
        kmla_ref[...] = jnp.concatenate(kb, axis=1).astype(BF16)
        vmla_ref[...] = kvup[:, MLA_HEADS * LANES:].astype(BF16)
    if sample:
        dk_ref[...] = kd
        kpe_ref[...] = kr[:, :MLA_ROPE]
    else:
        dk_ref[...] = kd.T
        kpe_ref[...] = kr.T[:MLA_ROPE, :]
    dv_ref[...] = vd
    ckv_ref[...] = ckv_n


def _project(x2d, shift, scale, tab, consts, wup, *, sample, rows_per_mod, tm):
    n, d = x2d.shape
    steps = n // tm
    mod_rows = shift.shape[1]
    tab_rows = tab.shape[0]
    tab_block = tm if tab_rows > 1 else 1
    tab_steps = max(tab_rows // tm, 1)
    mod_spec = pl.BlockSpec((None, mod_rows, d), lambda i: (i * tm // rows_per_mod, 0, 0))
    tab_spec = pl.BlockSpec((tab_block, LANES), lambda i: (i % tab_steps, 0))

    def full(a):
        nd = a.ndim
        return pl.BlockSpec(a.shape, lambda i: (0,) * nd)

    def rows(cols, dt):
        return jax.ShapeDtypeStruct((n, cols), dt), pl.BlockSpec((tm, cols), lambda i: (i, 0))

    if sample:
        outs = [rows(DIFF_Q_COLS, F32), rows(DIFF_K_COLS, F32), rows(DIFF_V_COLS, F32), rows(KV_LORA, F32),
                rows(MLA_ROPE, F32), rows(MLA_HEADS * KV_LORA, F32), rows(MLA_HEADS * LANES, F32)]
    else:
        per_seq = rows_per_mod // tm

        def cols_t(width):
            return (jax.ShapeDtypeStruct((n // rows_per_mod, width, rows_per_mod), F32),
                    pl.BlockSpec((None, width, tm), lambda i: (i // per_seq, 0, i % per_seq)))

        outs = [rows(2 * DIFF_Q_COLS, BF16), cols_t(DIFF_K_COLS), rows(DIFF_V_COLS, F32), rows(KV_LORA, F32),
                cols_t(MLA_ROPE), rows(2 * MLA_HEADS * LANES, BF16), rows(2 * MLA_HEADS * LANES, BF16),
                rows(MLA_HEADS * LANES, BF16)]
    out_shape = [o[0] for o in outs]
    out_specs = [o[1] for o in outs]
    in_specs = [pl.BlockSpec((tm, d), lambda i: (i, 0)), mod_spec, mod_spec, tab_spec] + [full(a) for a in consts] + [full(wup)]
    weight_bytes = sum(int(a.size) * a.dtype.itemsize for a in consts) + int(wup.size) * wup.dtype.itemsize
    vmem = 2 * weight_bytes + 2 * tm * d * 4 + 12 * tm * IN_COLS_EXT * 4
    return pl.pallas_call(
        functools.partial(_proj_kernel, sample=sample),
        out_shape=out_shape, grid=(steps,), in_specs=in_specs, out_specs=out_specs,
        compiler_params=_cparams(("arbitrary",), vmem),
        name="in_proj_sample" if sample else "in_proj_prompt",
    )(x2d, shift, scale, tab, *consts, wup)


def _softmax_step(s, v_bf, m, l, acc):
    m_new = jnp.maximum(m, jnp.max(s, axis=-1, keepdims=True))
    alpha = jnp.exp(m - m_new)
    p = jnp.exp(s - m_new)
    l_new = alpha * l + jnp.sum(p, axis=-1, keepdims=True)
    acc_new = alpha * acc + _dot(p.astype(BF16), v_bf)
    return m_new, l_new, acc_new


def _diff_lambda(lv, lam_init):
    a = jnp.sum(lv[0:1, :] * lv[1:2, :], axis=-1, keepdims=True)
    b = jnp.sum(lv[2:3, :] * lv[3:4, :], axis=-1, keepdims=True)
    return jnp.exp(a) - jnp.exp(b) + lam_init


def _causal_blocks(i, tq, tk):
    assert tq % tk == 0 or tk % tq == 0
    return (i * tq) // tk, max(tq // tk, 1)


def _chain_stats(s, m_old, l_old):
    m_new = jnp.maximum(m_old, jnp.max(s, axis=-1, keepdims=True))
    alpha = jnp.exp(m_old - m_new)
    p = jnp.exp(s - m_new)
    return m_new, alpha * l_old + jnp.sum(p, axis=-1, keepdims=True), p.astype(BF16), alpha


def _chain_init(n_chains, rows):
    return tuple((jnp.full((rows, 1), NEG_INF, F32), jnp.zeros((rows, 1), F32), jnp.zeros((rows, HEAD_DIM), F32))
                 for _ in range(n_chains))


def _diff_prompt_kernel(lam_ref, gsub_ref, q_ref, kt_ref, v_ref, o_ref, *, tq, tk, lam_init):
    kvh = pl.program_id(1)
    i = pl.program_id(2)
    q = q_ref[...]
    nrow = 2 * DIFF_GROUP
    qs = jnp.concatenate([q[:, c * LANES:(c + 1) * LANES] for c in range(nrow)], axis=0)
    kv_scale = jnp.where(kvh == 0, 1.0, 2.0 ** (-DIFF_GROUP)).astype(F32)
    rel = (lax.broadcasted_iota(I32, (tq, tk), 0) - lax.broadcasted_iota(I32, (tq, tk), 1))

    def step(j, carry, masked):
        m, l, acc = carry
        start = pl.multiple_of(j * tk, tk)
        kt = kt_ref[:, pl.ds(start, tk)].astype(BF16)
        v = v_ref[pl.ds(start, tk), :].astype(BF16)
        s = _dot(qs, kt)
        dist = rel + (i * tq - j * tk)
        distf = dist.astype(F32)
        pieces = []
        for g in range(DIFF_GROUP):
            bias = (2.0 ** (-(g + 1))) * kv_scale * distf
            for mp in range(2):
                r0 = (g * 2 + mp) * tq
                sg = s[r0:r0 + tq] - bias
                if masked:
                    sg = jnp.where(dist >= 0, sg, NEG_INF)
                pieces.append(sg)
        s = jnp.concatenate(pieces, axis=0)
        return _softmax_step(s, v, m, l, acc)

    init = (jnp.full((nrow * tq, 1), NEG_INF, F32), jnp.zeros((nrow * tq, 1), F32),
            jnp.zeros((nrow * tq, HEAD_DIM), F32))
    n_full, n_diag = _causal_blocks(i, tq, tk)
    carry = lax.fori_loop(0, n_full, lambda j, c: step(j, c, False), init)
    for jj in range(n_diag):
        carry = step(n_full + jj, carry, True)
    m, l, acc = carry
    o = acc / l
    lam = _diff_lambda(lam_ref[...], lam_init)
    gsub = gsub_ref[...] * (1.0 - lam_init)
    outs = []
    for g in range(DIFF_GROUP):
        og = o[(2 * g) * tq:(2 * g + 1) * tq] - lam * o[(2 * g + 1) * tq:(2 * g + 2) * tq]
        og = og * lax.rsqrt(jnp.mean(og * og, axis=-1, keepdims=True) + RMS_EPS) * gsub
        outs.append(og)
    o_ref[...] = jnp.concatenate(outs, axis=1).astype(o_ref.dtype)


def _diff_prompt_attention(qds, dk_t, dv, lam_v, gsub, *, batch, seq, lam_init):
    tq, tk = TQ_DIFF, TK_ATTN
    nq = seq // tq
    nrow = 2 * DIFF_GROUP
    width = nrow * LANES
    vmem = 2 * (tq * width * 2 + 2 * seq * LANES * 4) + 6 * nrow * tq * tk * 4 + 3 * nrow * tq * LANES * 4
    return pl.pallas_call(
        functools.partial(_diff_prompt_kernel, tq=tq, tk=tk, lam_init=lam_init),
        out_shape=jax.ShapeDtypeStruct((batch * seq, DIFF_HEADS * HEAD_DIM), BF16),
        grid=(batch, DIFF_KV_HEADS, nq),
        in_specs=[pl.BlockSpec(lam_v.shape, lambda b, h, i: (0, 0)),
                  pl.BlockSpec(gsub.shape, lambda b, h, i: (0, 0)),
                  pl.BlockSpec((tq, width), lambda b, h, i: (b * nq + i, h)),
                  pl.BlockSpec((None, LANES, seq), lambda b, h, i: (b, h, 0)),
                  pl.BlockSpec((seq, HEAD_DIM), lambda b, h, i: (b, h))],
        out_specs=pl.BlockSpec((tq, DIFF_GROUP * HEAD_DIM), lambda b, h, i: (b * nq + i, h)),
        compiler_params=_cparams(("arbitrary", "arbitrary", "arbitrary"), vmem),
        name="diff_attn_prompt",
    )(lam_v, gsub, qds, dk_t, dv)


def _mla_prompt_kernel(q_ref, k_ref, v_ref, o_ref, *, tq, tk, chain):
    i = pl.program_id(1)
    qk = 2 * LANES
    per_head = tq // chain
    rel = (lax.broadcasted_iota(I32, (chain, tk), 0) - lax.broadcasted_iota(I32, (chain, tk), 1))

    def step(j, state, masked):
        start = pl.multiple_of(j * tk, tk)
        scores = []
        for hh in range(MLA_HEADS):
            k = k_ref[pl.ds(start, tk), hh * qk:(hh + 1) * qk]
            for r in range(per_head):
                scores.append(_dot_nt(q_ref[r * chain:(r + 1) * chain, hh * qk:(hh + 1) * qk], k))
        stats = []
        for hh in range(MLA_HEADS):
            for r in range(per_head):
                c = hh * per_head + r
                s = scores[c]
                if masked:
                    s = jnp.where(rel + (i * tq + r * chain - j * tk) >= 0, s, NEG_INF)
                stats.append(_chain_stats(s, state[c][0], state[c][1]))
        new = []
        for hh in range(MLA_HEADS):
            v = v_ref[pl.ds(start, tk), hh * HEAD_DIM:(hh + 1) * HEAD_DIM]
            for r in range(per_head):
                c = hh * per_head + r
                m_new, l_new, p, alpha = stats[c]
                new.append((m_new, l_new, alpha * state[c][2] + _dot(p, v)))
        return tuple(new)

    n_full, n_diag = _causal_blocks(i, tq, tk)
    state = lax.fori_loop(0, n_full, lambda j, st: step(j, st, False), _chain_init(MLA_HEADS * per_head, chain))
    for jj in range(n_diag):
        state = step(n_full + jj, state, True)
    for hh in range(MLA_HEADS):
        o_h = jnp.concatenate([state[hh * per_head + r][2] / state[hh * per_head + r][1] for r in range(per_head)],
                              axis=0)
        o_ref[:, hh * HEAD_DIM:(hh + 1) * HEAD_DIM] = o_h.astype(o_ref.dtype)


def _mla_prompt_attention(qmla, kmla, vmla, *, batch, seq):
    tq, tk, chain = TQ_MLA, TK_ATTN, CHAIN_MLA
    nq = seq // tq
    qk = 2 * LANES
    vmem = (2 * (tq * MLA_HEADS * qk * 2 + seq * MLA_HEADS * qk * 2 + seq * MLA_HEADS * HEAD_DIM * 2
                 + tq * MLA_HEADS * HEAD_DIM * 2) + 3 * MLA_HEADS * tq * LANES * 4 + 16 * chain * tk * 4)
    return pl.pallas_call(
        functools.partial(_mla_prompt_kernel, tq=tq, tk=tk, chain=chain),
        out_shape=jax.ShapeDtypeStruct((batch * seq, MLA_HEADS * HEAD_DIM), BF16),
        grid=(batch, nq),
        in_specs=[pl.BlockSpec((tq, MLA_HEADS * qk), lambda b, i: (b * nq + i, 0)),
                  pl.BlockSpec((seq, MLA_HEADS * qk), lambda b, i: (b, 0)),
                  pl.BlockSpec((seq, MLA_HEADS * HEAD_DIM), lambda b, i: (b, 0))],
        out_specs=pl.BlockSpec((tq, MLA_HEADS * HEAD_DIM), lambda b, i: (b * nq + i, 0)),
        compiler_params=_cparams(("arbitrary", "arbitrary"), vmem),
        name="mla_attn_prompt",
    )(qmla, kmla, vmla)


def _decode_kernel(pt_ref, lam_ref, gsub_ref, slope_ref, qlat_ref, qpe_ref, ckvn_ref, kpen_ref, qdm_ref,
                   kdn_ref, vdn_ref, c_ckv, c_kpe, c_dk, c_dv, olat_ref, od_ref,
                   ckv_buf, kpe_buf, dk_buf, dv_buf, sem, *, n_seq, n_chunks, pages, lam_init):
    s_idx = pl.program_id(0)
    chunk_len = pages * PAGE_SIZE

    def copies(seq, chunk, slot):
        out = []
        for p in range(pages):
            page = pt_ref[seq, chunk * pages + p]
            out.append(pltpu.make_async_copy(
                c_ckv.at[page], ckv_buf.at[slot, pl.ds(p * PAGE_SIZE, PAGE_SIZE), :], sem.at[slot]))
            out.append(pltpu.make_async_copy(
                c_kpe.at[page], kpe_buf.at[slot, :, pl.ds(p * PAGE_SIZE, PAGE_SIZE)], sem.at[slot]))
            out.append(pltpu.make_async_copy(
                c_dk.at[page], dk_buf.at[slot, :, pl.ds(p * PAGE_SIZE, PAGE_SIZE)], sem.at[slot]))
            out.append(pltpu.make_async_copy(
                c_dv.at[page], dv_buf.at[slot, pl.ds(2 * p * PAGE_SIZE, 2 * PAGE_SIZE), :], sem.at[slot]))
        return out

    def start_chunk(seq, chunk, slot):
        for n, cp in enumerate(copies(seq, chunk, slot)):
            cp.start(priority=min(n % 4, 1))

    def wait_chunk(seq, chunk, slot):
        for cp in copies(seq, chunk, slot):
            cp.wait()

    @pl.when(s_idx == 0)
    def _():
        start_chunk(0, 0, 0)

    qlat = qlat_ref[...]
    qpe = qpe_ref[...]
    qdm = qdm_ref[...]
    qlat_bf, qpe_bf, qdm_bf = qlat.astype(BF16), qpe.astype(BF16), qdm.astype(BF16)
    ckv_new = ckvn_ref[...]
    slopes = slope_ref[...][:, 0:1]
    past_len = n_chunks * chunk_len

    m_m = (jnp.sum(qlat * ckv_new, axis=-1, keepdims=True)
           + jnp.sum(qpe * kpen_ref[...], axis=-1, keepdims=True))
    l_m = jnp.ones_like(m_m)
    acc_m = jnp.broadcast_to(ckv_new, (MLA_HEADS, KV_LORA)).astype(F32)
    m_d = jnp.sum(qdm * kdn_ref[...], axis=-1, keepdims=True)
    l_d = jnp.ones_like(m_d)
    vdn = vdn_ref[...]
    half = 2 * DIFF_GROUP
    acc_d = jnp.concatenate([jnp.broadcast_to(vdn[:, :HEAD_DIM], (half, HEAD_DIM)),
                             jnp.broadcast_to(vdn[:, HEAD_DIM:], (half, HEAD_DIM))], axis=0).astype(F32)

    def body(c, carry):
        m_m, l_m, acc_m, m_d, l_d, acc_d = carry
        step = s_idx * n_chunks + c
        slot = step % 2

        @pl.when(c + 1 < n_chunks)
        def _():
            start_chunk(s_idx, c + 1, 1 - slot)

        @pl.when(jnp.logical_and(c + 1 == n_chunks, s_idx + 1 < n_seq))
        def _():
            start_chunk(s_idx + 1, 0, 1 - slot)

        wait_chunk(s_idx, c, slot)

        ckv = ckv_buf[slot].astype(BF16)
        s_m = _dot_nt(qlat_bf, ckv) + _dot(qpe_bf, kpe_buf[slot].astype(BF16))
        m_m, l_m, acc_m = _softmax_step(s_m, ckv, m_m, l_m, acc_m)

        s_d = _dot(qdm_bf, dk_buf[slot].astype(BF16))
        pos = c * chunk_len + _lane_iota((1, chunk_len))
        s_d = s_d - slopes * (past_len - pos).astype(F32)
        m_new = jnp.maximum(m_d, jnp.max(s_d, axis=-1, keepdims=True))
        alpha = jnp.exp(m_d - m_new)
        p = jnp.exp(s_d - m_new)
        l_d = alpha * l_d + jnp.sum(p, axis=-1, keepdims=True)
        p_bf = p.astype(BF16)
        v0 = dv_buf[slot, pl.ds(0, chunk_len, stride=2), :].astype(BF16)
        v1 = dv_buf[slot, pl.ds(1, chunk_len, stride=2), :].astype(BF16)
        pv = jnp.concatenate([_dot(p_bf[:half], v0), _dot(p_bf[half:], v1)], axis=0)
        acc_d = alpha * acc_d + pv
        return m_m, l_m, acc_m, m_new, l_d, acc_d

    m_m, l_m, acc_m, m_d, l_d, acc_d = lax.fori_loop(0, n_chunks, body, (m_m, l_m, acc_m, m_d, l_d, acc_d))

    olat_ref[...] = acc_m / l_m
    o = acc_d / l_d
    lam = _diff_lambda(lam_ref[...], lam_init)
    g4 = DIFF_GROUP
    od = jnp.concatenate([o[0:g4] - lam * o[g4:2 * g4], o[2 * g4:3 * g4] - lam * o[3 * g4:4 * g4]], axis=0)
    od = od * lax.rsqrt(jnp.mean(od * od, axis=-1, keepdims=True) + RMS_EPS) * (gsub_ref[...] * (1.0 - lam_init))
    od_ref[...] = od


def _decode_attention(page_table, lam_v, gsub, slopes, qlat, qpe, ckv_new, kpe_new, qdm, kd_new, vd_new,
                      c_ckv, c_kpe, c_dk, c_dv, *, lam_init):
    n_seq, n_pages = page_table.shape
    pages = min(PAGES_PER_CHUNK, n_pages)
    n_chunks = n_pages // pages
    chunk_len = pages * PAGE_SIZE

    def per_seq(a):
        return pl.BlockSpec((None,) + a.shape[1:], lambda s, pt: (s,) + (0,) * (a.ndim - 1))

    def full(a):
        return pl.BlockSpec(a.shape, lambda s, pt: (0,) * a.ndim)

    any_spec = pl.BlockSpec(memory_space=pl.ANY)
    buf_bytes = 2 * chunk_len * (KV_LORA + MLA_ROPE + DIFF_K_COLS + DIFF_V_COLS) * 4
    grid_spec = pltpu.PrefetchScalarGridSpec(
        num_scalar_prefetch=1,
        grid=(n_seq,),
        in_specs=[full(lam_v), full(gsub), full(slopes), per_seq(qlat), per_seq(qpe), per_seq(ckv_new),
                  per_seq(kpe_new), per_seq(qdm), per_seq(kd_new), per_seq(vd_new),
                  any_spec, any_spec, any_spec, any_spec],
        out_specs=[pl.BlockSpec((None, MLA_HEADS, KV_LORA), lambda s, pt: (s, 0, 0)),
                   pl.BlockSpec((None, DIFF_HEADS, HEAD_DIM), lambda s, pt: (s, 0, 0))],
        scratch_shapes=[pltpu.VMEM((2, chunk_len, KV_LORA), F32),
                        pltpu.VMEM((2, MLA_ROPE, chunk_len), F32),
                        pltpu.VMEM((2, DIFF_K_COLS, chunk_len), F32),
                        pltpu.VMEM((2, 2 * chunk_len, HEAD_DIM), F32),
                        pltpu.SemaphoreType.DMA((2,))],
    )
    return pl.pallas_call(
        functools.partial(_decode_kernel, n_seq=n_seq, n_chunks=n_chunks, pages=pages, lam_init=lam_init),
        out_shape=[jax.ShapeDtypeStruct((n_seq, MLA_HEADS, KV_LORA), F32),
                   jax.ShapeDtypeStruct((n_seq, DIFF_HEADS, HEAD_DIM), F32)],
        grid_spec=grid_spec,
        compiler_params=_cparams(("arbitrary",), buf_bytes + buf_bytes // 2),
        name="decode_attn",
    )(page_table, lam_v, gsub, slopes, qlat, qpe, ckv_new, kpe_new, qdm, kd_new, vd_new, c_ckv, c_kpe, c_dk, c_dv)


def _latent_out_kernel(o_ref, w_ref, out_ref):
    out_ref[...] = _dot(o_ref[...].astype(BF16), w_ref[...]).astype(out_ref.dtype)


def _latent_out(olat2d, w_uv_bf):
    n = olat2d.shape[0]
    return pl.pallas_call(
        _latent_out_kernel,
        out_shape=jax.ShapeDtypeStruct((n, MLA_HEADS * HEAD_DIM), BF16),
        grid=(MLA_HEADS,),
        in_specs=[pl.BlockSpec((n, KV_LORA), lambda h: (0, h)),
                  pl.BlockSpec((None, KV_LORA, HEAD_DIM), lambda h: (h, 0, 0))],
        out_specs=pl.BlockSpec((n, HEAD_DIM), lambda h: (0, h)),
        compiler_params=_cparams(("arbitrary",), 4 * n * KV_LORA * 4),
        name="latent_out",
    )(olat2d, w_uv_bf)


def _pack_bf16_pairs(a, b):
    ua = pltpu.bitcast(a.astype(BF16).astype(F32), U32) & jnp.uint32(0xFFFF0000)
    ub = pltpu.bitcast(b.astype(BF16).astype(F32), U32) >> jnp.uint32(16)
    return ua | ub


def _unpack_bf16_pairs(u):
    a = pltpu.bitcast(u & jnp.uint32(0xFFFF0000), F32)
    b = pltpu.bitcast(u << jnp.uint32(16), F32)
    return a, b


def _outproj_kernel(ad_ref, am_ref, x_ref, gt_ref, sh_ref, sc_ref, g2_ref, wo_ref, wr_ref, br_ref, tri_ref, cin_ref,
                    x1_ref, h2_ref, slab_ref, cout_ref, carry_ref):
    i = pl.program_id(0)

    @pl.when(i == 0)
    def _():
        carry_ref[...] = cin_ref[...]

    half = wo_ref.shape[0] // 2
    mix = _dot(ad_ref[...], wo_ref[:half, :]) + _dot(am_ref[...], wo_ref[half:, :])
    x1 = x_ref[...] + gt_ref[...] * mix
    x1_ref[...] = x1
    h = x1 * lax.rsqrt(jnp.mean(x1 * x1, axis=-1, keepdims=True) + RMS_EPS) * g2_ref[...]
    h = h * (1.0 + sc_ref[...]) + sh_ref[...]
    d = h.shape[1]
    h2_ref[...] = _pack_bf16_pairs(h[:, :d // 2], h[:, d // 2:])

    h_hi = h.astype(BF16)
    h_lo = (h - h_hi.astype(F32)).astype(BF16)
    r2 = _dot(jnp.concatenate([h_hi, h_lo], axis=1), wr_ref[...])
    ne = r2.shape[1] // 2
    logits = r2[:, :ne] + r2[:, ne:] + br_ref[...]
    tm = logits.shape[0]
    lane = _lane_iota((tm, ne)).astype(F32)
    slab_lane = _lane_iota((tm, LANES))
    carry = carry_ref[...]
    work = logits
    onehots, vals, idxs = [], [], []
    for _ in range(TOP_K):
        mx = jnp.max(work, axis=-1, keepdims=True)
        ix = jnp.min(jnp.where(work == mx, lane, float(ne)), axis=-1, keepdims=True)
        oh = lane == ix
        onehots.append(oh)
        vals.append(mx)
        idxs.append(ix)
        work = jnp.where(oh, -jnp.inf, work)
    cnt = sum(jnp.where(oh, 1.0, 0.0) for oh in onehots)
    before = _dot(tri_ref[...], cnt.astype(BF16)) + carry
    exps = [jnp.exp(v - vals[0]) for v in vals]
    den = sum(exps)
    slab = jnp.zeros((tm, LANES), F32)
    for k in range(TOP_K):
        rank = jnp.sum(jnp.where(onehots[k], before, 0.0), axis=-1, keepdims=True)
        slab = jnp.where(slab_lane == k, idxs[k], slab)
        slab = jnp.where(slab_lane == TOP_K + k, rank, slab)
        slab = jnp.where(slab_lane == 2 * TOP_K + k, exps[k] / den, slab)
    slab_ref[...] = slab
    new_carry = carry + jnp.sum(cnt, axis=0, keepdims=True)
    carry_ref[...] = new_carry
    cout_ref[...] = new_carry


def _out_project(attn_d, attn_m, x2d, gate, shift, scale, consts, count_in, *, rows_per_mod, tm, name):
    n, d = x2d.shape
    steps = n // tm
    mod_rows = gate.shape[1]
    mod_spec = pl.BlockSpec((None, mod_rows, d), lambda i: (i * tm // rows_per_mod, 0, 0))

    def full(a):
        nd = a.ndim
        return pl.BlockSpec(a.shape, lambda i: (0,) * nd)

    in_specs = [pl.BlockSpec((tm, d // 2), lambda i: (i, 0)), pl.BlockSpec((tm, d // 2), lambda i: (i, 0)),
                pl.BlockSpec((tm, d), lambda i: (i, 0)), mod_spec, mod_spec, mod_spec] + [full(a) for a in consts] + [full(count_in)]
    out_shape = [jax.ShapeDtypeStruct((n, d), F32), jax.ShapeDtypeStruct((n, d // 2), U32),
                 jax.ShapeDtypeStruct((n, LANES), F32), jax.ShapeDtypeStruct(count_in.shape, F32)]
    out_specs = [pl.BlockSpec((tm, d), lambda i: (i, 0)), pl.BlockSpec((tm, d // 2), lambda i: (i, 0)),
                 pl.BlockSpec((tm, LANES), lambda i: (i, 0)), full(count_in)]
    weight_bytes = sum(int(a.size) * a.dtype.itemsize for a in consts)
    return pl.pallas_call(
        _outproj_kernel, out_shape=out_shape, grid=(steps,), in_specs=in_specs, out_specs=out_specs,
        scratch_shapes=[pltpu.VMEM(count_in.shape, F32)],
        compiler_params=_cparams(("arbitrary",), 2 * weight_bytes + 16 * tm * d * 4),
        name=name,
    )(attn_d, attn_m, x2d, gate, shift, scale, *consts, count_in)


ZERO_CHUNK = 128


def _zero_row_ranges(lo_ref, hi_ref, zbuf, xs_out, sem, n_ranges):
    def pieces(g):
        lo, hi = lo_ref[g], hi_ref[g]
        n1 = (-lo) & 7
        a8 = lo + n1
        n8 = ((-a8) & (ZERO_CHUNK - 1)) >> 3
        a_big = a8 + 8 * n8
        n_big = (hi - a_big) // ZERO_CHUNK
        return ((n1, 1, lambda t: lo + t),
                (n8, 8, lambda t: pl.multiple_of(a8 + 8 * t, 8)),
                (n_big, ZERO_CHUNK, lambda t: pl.multiple_of(a_big + ZERO_CHUNK * t, ZERO_CHUNK)))

    def for_all(action):
        def per_range(g, c):
            for count, rows, row_of in pieces(g):
                def one(t, cc, rows=rows, row_of=row_of):
                    action(pltpu.make_async_copy(zbuf.at[pl.ds(0, rows), :], xs_out.at[pl.ds(row_of(t), rows), :], sem))
                    return cc
                lax.fori_loop(0, count, one, 0)
            return c
        lax.fori_loop(0, n_ranges, per_range, 0)

    for_all(lambda cp: cp.start())
    for_all(lambda cp: cp.wait())


def _dispatch_kernel(zlo_ref, zhi_ref, dest_ref, hp_ref, hs_ref, xs_out, zbuf, sem, *, tm, prompt_steps, n_ranges):
    @pl.when(pl.program_id(0) == 0)
    def _():
        zbuf[...] = jnp.zeros(zbuf.shape, zbuf.dtype)
        _zero_row_ranges(zlo_ref, zhi_ref, zbuf, xs_out, sem, n_ranges)

    def scatter_rows(h_ref):
        def row_copy(r, k):
            return pltpu.make_async_copy(h_ref.at[pl.ds(r, 1), :],
                                         xs_out.at[pl.ds(dest_ref[r * TOP_K + k], 1), :], sem)

        def issue(r, c):
            for k in range(TOP_K):
                row_copy(r, k).start(priority=k % 2)
            return c

        def drain(r, c):
            for k in range(TOP_K):
                row_copy(r, k).wait()
            return c

        lax.fori_loop(0, tm, issue, 0, unroll=DMA_UNROLL)
        lax.fori_loop(0, tm, drain, 0, unroll=DMA_UNROLL)

    @pl.when(pl.program_id(0) < prompt_steps)
    def _():
        scatter_rows(hp_ref)

    @pl.when(pl.program_id(0) >= prompt_steps)
    def _():
        scatter_rows(hs_ref)


def _dispatch(h2_p, h2_s, dest_flat, zero_lo, zero_hi, n_rows):
    tm = TM_TOK
    w = h2_p.shape[1]
    p_steps, s_steps = h2_p.shape[0] // tm, h2_s.shape[0] // tm
    grid_spec = pltpu.PrefetchScalarGridSpec(
        num_scalar_prefetch=2,
        grid=(p_steps + s_steps,),
        in_specs=[pl.BlockSpec((tm * TOP_K,), lambda i, zl, zh: (i,), memory_space=pltpu.SMEM),
                  pl.BlockSpec((tm, w), lambda i, zl, zh: (jnp.minimum(i, p_steps - 1), 0)),
                  pl.BlockSpec((tm, w), lambda i, zl, zh: (jnp.maximum(i - p_steps, 0), 0))],
        out_specs=pl.BlockSpec(memory_space=pl.ANY),
        scratch_shapes=[pltpu.VMEM((ZERO_CHUNK, w), U32), pltpu.SemaphoreType.DMA(())])
    return pl.pallas_call(
        functools.partial(_dispatch_kernel, tm=tm, prompt_steps=p_steps, n_ranges=zero_lo.shape[0]),
        out_shape=jax.ShapeDtypeStruct((n_rows, w), U32),
        grid_spec=grid_spec,
        compiler_params=pltpu.CompilerParams(dimension_semantics=("arbitrary",), has_side_effects=True),
        name="moe_dispatch",
    )(zero_lo, zero_hi, dest_flat, h2_p, h2_s)


def _expert_changed(be_ref, i):
    prev = be_ref[jnp.maximum(i - 1, 0)]
    return jnp.logical_or(i == 0, be_ref[i] != prev)


def _by_filled_half(i, nu_ref, rv_ref, out_ref, compute):
    tb = out_ref.shape[0]
    half = tb // 2
    active = i < nu_ref[0]
    filled = rv_ref[jnp.minimum(i, nu_ref[0] - 1)]

    @pl.when(jnp.logical_and(active, filled > half))
    def _():
        compute(slice(0, tb))

    @pl.when(jnp.logical_and(active, filled <= half))
    def _():
        compute(slice(0, half))
        out_ref[half:, :] = jnp.zeros((tb - half, out_ref.shape[1]), out_ref.dtype)

    @pl.when(jnp.logical_not(active))
    def _():
        out_ref[...] = jnp.zeros_like(out_ref)


def _gate_up_kernel(be_ref, nu_ref, rv_ref, xs_ref, wg_ref, wu_ref, bg_ref, bu_ref, act_ref, wg_bf, wu_bf):
    i = pl.program_id(1)

    @pl.when(jnp.logical_and(i < nu_ref[0], _expert_changed(be_ref, i)))
    def _():
        wg_bf[...] = wg_ref[...].astype(BF16)
        wu_bf[...] = wu_ref[...].astype(BF16)

    def compute(rows):
        a, b = _unpack_bf16_pairs(xs_ref[rows, :])
        x = jnp.concatenate([a, b], axis=1).astype(BF16)
        gate = jnp.minimum(_dot(x, wg_bf[...]) + bg_ref[...], SWIGLU_LIMIT)
        up = jnp.clip(_dot(x, wu_bf[...]) + bu_ref[...], -SWIGLU_LIMIT, SWIGLU_LIMIT)
        act_ref[rows, :] = ((up + 1.0) * gate * jax.nn.sigmoid(SWIGLU_ALPHA * gate)).astype(act_ref.dtype)

    _by_filled_half(i, nu_ref, rv_ref, act_ref, compute)


def _down_kernel(be_ref, nu_ref, rv_ref, act_ref, wa_ref, wb_ref, ba_ref, bb_ref, y_ref, wa_bf, wb_bf):
    i = pl.program_id(1)

    @pl.when(jnp.logical_and(i < nu_ref[0], _expert_changed(be_ref, i)))
    def _():
        wa_bf[...] = wa_ref[...].astype(BF16)
        wb_bf[...] = wb_ref[...].astype(BF16)

    def compute(rows):
        act = act_ref[rows, :]
        ya = _dot(act, wa_bf[...]) + ba_ref[...]
        yb = _dot(act, wb_bf[...]) + bb_ref[...]
        y_ref[rows, :] = _pack_bf16_pairs(ya, yb)

    _by_filled_half(i, nu_ref, rv_ref, y_ref, compute)


def _experts(xs, block_exp, n_used, rows_filled, w_gu, b_gu, w_down, b_down):
    n_rows, half_d = xs.shape
    d = 2 * half_d
    ne, _, two_f = w_gu.shape
    f = two_f // 2
    tb, tn, tn2 = TB_MOE, TN_GU, TN_DOWN
    n_blk = n_rows // tb
    b_gu3 = b_gu.reshape(ne, 1, two_f)
    b_down3 = b_down.reshape(ne, 1, d)

    def grid_spec(in_cols, k_dim, tile, second_off):
        def blk(i, nu):
            return jnp.minimum(i, nu[0] - 1)

        def w_spec(rows, off):
            return pl.BlockSpec((None, rows, tile), lambda j, i, be, nu, rv: (be[blk(i, nu)], 0, j + off))

        return pltpu.PrefetchScalarGridSpec(
            num_scalar_prefetch=3, grid=(second_off, n_blk),
            in_specs=[pl.BlockSpec((tb, in_cols), lambda j, i, be, nu, rv: (blk(i, nu), 0)),
                      w_spec(k_dim, 0), w_spec(k_dim, second_off), w_spec(1, 0), w_spec(1, second_off)],
            out_specs=pl.BlockSpec((tb, tile), lambda j, i, be, nu, rv: (i, j)),
            scratch_shapes=[pltpu.VMEM((k_dim, tile), BF16), pltpu.VMEM((k_dim, tile), BF16)])

    act = pl.pallas_call(
        _gate_up_kernel, out_shape=jax.ShapeDtypeStruct((n_rows, f), BF16),
        grid_spec=grid_spec(half_d, d, tn, f // tn),
        compiler_params=_cparams(("arbitrary", "arbitrary"), 4 * d * tn * 4 + 2 * d * tn * 2 + 8 * tb * d * 4),
        name="moe_gate_up",
    )(block_exp, n_used, rows_filled, xs, w_gu, w_gu, b_gu3, b_gu3)

    return pl.pallas_call(
        _down_kernel, out_shape=jax.ShapeDtypeStruct((n_rows, half_d), U32),
        grid_spec=grid_spec(f, f, tn2, half_d // tn2),
        compiler_params=_cparams(("arbitrary", "arbitrary"), 4 * f * tn2 * 4 + 2 * f * tn2 * 2 + 8 * tb * f * 2),
        name="moe_down",
    )(block_exp, n_used, rows_filled, act, w_down, w_down, b_down3, b_down3)


def _combine_kernel(dest_ref, x1_ref, gt_ref, slab_ref, yb_ref, y_ref, buf, sem, *, tm):
    def row_copy(r, k):
        return pltpu.make_async_copy(yb_ref.at[pl.ds(dest_ref[r * TOP_K + k], 1), :],
                                     buf.at[k, pl.ds(r, 1), :], sem)

    def issue(r, c):
        for k in range(TOP_K):
            row_copy(r, k).start(priority=k % 2)
        return c

    def drain(r, c):
        for k in range(TOP_K):
            row_copy(r, k).wait()
        return c

    lax.fori_loop(0, tm, issue, 0, unroll=DMA_UNROLL)
    lax.fori_loop(0, tm, drain, 0, unroll=DMA_UNROLL)
    slab = slab_ref[...]
    acc_a = None
    for k in range(TOP_K):
        a, b = _unpack_bf16_pairs(buf[k])
        g = slab[:, 2 * TOP_K + k:2 * TOP_K + k + 1]
        acc_a = g * a if acc_a is None else acc_a + g * a
        acc_b = g * b if k == 0 else acc_b + g * b
    moe = jnp.concatenate([acc_a, acc_b], axis=1)
    y_ref[...] = x1_ref[...] + gt_ref[...] * moe


def _combine(x1, gate, slab, dest_flat, yb, *, rows_per_mod, tm):
    n, d = x1.shape
    mod_rows = gate.shape[1]
    return pl.pallas_call(
        functools.partial(_combine_kernel, tm=tm),
        out_shape=jax.ShapeDtypeStruct((n, d), F32),
        grid=(n // tm,),
        in_specs=[pl.BlockSpec((tm * TOP_K,), lambda i: (i,), memory_space=pltpu.SMEM),
                  pl.BlockSpec((tm, d), lambda i: (i, 0)),
                  pl.BlockSpec((None, mod_rows, d), lambda i: (i * tm // rows_per_mod, 0, 0)),
                  pl.BlockSpec((tm, LANES), lambda i: (i, 0)),
                  pl.BlockSpec(memory_space=pl.ANY)],
        out_specs=pl.BlockSpec((tm, d), lambda i: (i, 0)),
        scratch_shapes=[pltpu.VMEM((TOP_K, tm, d // 2), U32), pltpu.SemaphoreType.DMA(())],
        compiler_params=_cparams(("arbitrary",), 8 * tm * d * 4),
        name="moe_combine",
    )(dest_flat, x1, gate, slab, yb)


def _rope_table(pos):
    inv = ROPE_THETA ** (-jnp.arange(0, MLA_ROPE, 2, dtype=F32) / MLA_ROPE)
    ang = pos.astype(F32)[:, None] * inv[None, :]
    c, s = jnp.cos(ang), jnp.sin(ang)
    return jnp.concatenate([c, c, -s, s], axis=-1)


def _swap_halves(a):
    h = a.shape[-1] // 2
    return jnp.concatenate([a[..., h:], a[..., :h]], axis=-1)


def kernel(x_prompt, x_sample, cache_dk, cache_dv, cache_ckv, cache_kpe, page_table, c_prompt, c_sample, norm1_g, norm2_g, w_ada, b_ada, w_in, diff_qn_g, diff_kn_g, diff_lambda, diff_subln_g, mla_qa_g, w_qb, mla_qn_g, mla_kva_g, mla_kpe_g, w_uk, w_uv, w_out, w_router, b_router, w_gu, b_gu, w_down, b_down):
    depth = w_in.shape[0]
    assert depth == 1, "single-layer trunk"
    batch, seq, d = x_prompt.shape
    n_seq, dec_seq, _ = x_sample.shape
    assert dec_seq == 1
    n_pool = cache_ckv.shape[1]
    n_pages = page_table.shape[1]
    past_len = n_pages * PAGE_SIZE
    n_p = batch * seq
    n_all = n_p + n_seq
    lam_init = 0.8 - 0.6 * math.exp(-0.3 * 0)
    assert seq % TM_PROJ == 0 and seq % TQ_MLA == 0 and seq % TK_ATTN == 0 and n_p % TM_TOK == 0 and n_seq % TM_TOK == 0
    assert TB_MOE % ZERO_CHUNK == 0

    w_in0 = w_in[0]
    w_in_ext = jnp.concatenate([w_in0, _swap_halves(w_in0[:, O_KPE:])], axis=1).astype(BF16)
    gmat = (jnp.arange(LANES)[:, None] // DIFF_DH == jnp.arange(LANES)[None, :] // DIFF_DH).astype(BF16)
    gdq = (jnp.tile(diff_qn_g[0].reshape(-1), DIFF_HEADS) * (DIFF_DH ** -0.5)).reshape(1, -1)
    gdk = jnp.tile(diff_kn_g[0].reshape(-1), DIFF_KV_HEADS).reshape(1, -1)
    gkpe = jnp.concatenate([mla_kpe_g[0], _swap_halves(mla_kpe_g[0])]).reshape(1, -1)
    qscale = MLA_QK_DIM ** -0.5
    gqn = (mla_qn_g[0, :MLA_NOPE] * qscale).reshape(1, -1)
    gq_pe = mla_qn_g[0, MLA_NOPE:]
    gqp = (jnp.concatenate([gq_pe, _swap_halves(gq_pe)]) * qscale).reshape(1, -1)
    wqb3 = w_qb[0].reshape(Q_LORA, MLA_HEADS, MLA_QK_DIM)
    wqb_pe = wqb3[:, :, MLA_NOPE:]
    wqb_ext = jnp.concatenate([wqb3[:, :, :MLA_NOPE].reshape(Q_LORA, -1),
                               jnp.concatenate([wqb_pe, _swap_halves(wqb_pe)], axis=-1).reshape(Q_LORA, -1)],
                              axis=1).astype(BF16)
    w_kvup = jnp.concatenate([jnp.transpose(w_uk[0], (1, 0, 2)).reshape(KV_LORA, -1),
                              jnp.transpose(w_uv[0], (1, 0, 2)).reshape(KV_LORA, -1)], axis=1).astype(BF16)
    w_uk_t = jnp.transpose(w_uk[0], (0, 2, 1)).astype(BF16)
    w_uv_bf = w_uv[0].astype(BF16)
    proj_consts = [norm1_g[0].reshape(1, -1), w_in_ext, gmat, gdq, gdk, mla_qa_g[0].reshape(1, -1),
                   mla_kva_g[0].reshape(1, -1), gkpe, wqb_ext, gqn, gqp]
    lam_v = diff_lambda[0]
    gsub = diff_subln_g[0].reshape(1, -1)

    mod = _modulation(jnp.concatenate([c_prompt, c_sample], axis=0), w_ada[0], b_ada[0])
    mod_p = mod[:batch].reshape(batch, 6, 1, d)
    mod_s = mod[batch:].reshape(n_seq, 6, d)
    sh1_p, sc1_p, gt1_p, sh2_p, sc2_p, gt2_p = [mod_p[:, t] for t in range(6)]
    sh1_s, sc1_s, gt1_s, sh2_s, sc2_s, gt2_s = [mod_s[:, t][None] for t in range(6)]

    xp2 = x_prompt.reshape(n_p, d)
    xs2 = x_sample.reshape(n_seq, d)
    tab_p = _rope_table(jnp.arange(seq, dtype=I32))
    tab_s = _rope_table(jnp.full((1,), past_len, I32))
    (qds, dk_p, dv_p, ckv_p, kpe_p, qmla, kmla, vmla) = _project(
        xp2, sh1_p, sc1_p, tab_p, proj_consts, w_kvup, sample=False, rows_per_mod=seq, tm=TM_PROJ)
    (qd_s, dk_s, dv_s, ckv_s, kpe_s, qlat_s, qpe_s) = _project(
        xs2, sh1_s, sc1_s, tab_s, proj_consts, w_uk_t, sample=True, rows_per_mod=n_seq, tm=n_seq)

    od_p = _diff_prompt_attention(qds, dk_p, dv_p, lam_v, gsub, batch=batch, seq=seq, lam_init=lam_init)
    om_p = _mla_prompt_attention(qmla, kmla, vmla, batch=batch, seq=seq)

    c_ckv = cache_ckv.reshape(n_pool, PAGE_SIZE, KV_LORA)
    c_kpe = jnp.transpose(cache_kpe, (0, 1, 3, 2)).reshape(n_pool, MLA_ROPE, PAGE_SIZE)
    c_dk = jnp.transpose(cache_dk, (0, 1, 3, 4, 5, 2)).reshape(n_pool, DIFF_K_COLS, PAGE_SIZE)
    c_dv = cache_dv.reshape(n_pool, PAGE_SIZE * DIFF_KV_HEADS, HEAD_DIM)
    q5 = qd_s.reshape(n_seq, DIFF_KV_HEADS, DIFF_GROUP, 2, DIFF_DH)
    eye_kv = jnp.eye(DIFF_KV_HEADS, dtype=F32)
    eye_m = jnp.eye(2, dtype=F32)
    qdm = jnp.einsum("skgmd,kK,mM->skmgKMd", q5, eye_kv, eye_m).reshape(n_seq, 2 * DIFF_HEADS, DIFF_K_COLS)
    head_id = jnp.arange(2 * DIFF_HEADS) // (2 * DIFF_GROUP) * DIFF_GROUP + jnp.arange(2 * DIFF_HEADS) % DIFF_GROUP
    slopes = jnp.broadcast_to((2.0 ** (-(head_id + 1).astype(F32)))[:, None], (2 * DIFF_HEADS, LANES))
    qpe3 = qpe_s.reshape(n_seq, MLA_HEADS, LANES)[:, :, :MLA_ROPE]
    olat_s, od_s = _decode_attention(
        page_table, lam_v, gsub, slopes, qlat_s.reshape(n_seq, MLA_HEADS, KV_LORA), qpe3,
        ckv_s.reshape(n_seq, 1, KV_LORA), kpe_s.reshape(n_seq, 1, MLA_ROPE), qdm,
        dk_s.reshape(n_seq, 1, DIFF_K_COLS), dv_s.reshape(n_seq, 1, DIFF_V_COLS),
        c_ckv, c_kpe, c_dk, c_dv, lam_init=lam_init)
    om_s = _latent_out(olat_s.reshape(n_seq, MLA_HEADS * KV_LORA), w_uv_bf)
    od_s = od_s.reshape(n_seq, DIFF_HEADS * HEAD_DIM).astype(BF16)

    tri = (jnp.arange(TM_PROJ)[:, None] > jnp.arange(TM_PROJ)[None, :]).astype(BF16)
    wr_hi = w_router[0].astype(BF16)
    wr_lo = (w_router[0] - wr_hi.astype(F32)).astype(BF16)
    wr_split = jnp.concatenate([jnp.concatenate([wr_hi, wr_lo], axis=1),
                                jnp.concatenate([wr_hi, jnp.zeros_like(wr_lo)], axis=1)], axis=0)
    out_consts = [norm2_g[0].reshape(1, -1), w_out[0].astype(BF16), wr_split, b_router[0].reshape(1, -1)]
    zero_counts = jnp.zeros((1, N_EXPERTS), F32)
    x1_p, h2_p, slab_p, counts_p = _out_project(
        od_p, om_p, xp2, gt1_p, sh2_p, sc2_p, out_consts + [tri], zero_counts,
        rows_per_mod=seq, tm=TM_PROJ, name="out_proj_router_prompt")
    x1_s, h2_s, slab_s, counts = _out_project(
        od_s, om_s, xs2, gt1_s, sh2_s, sc2_s, out_consts + [tri[:n_seq, :n_seq]], counts_p,
        rows_per_mod=n_seq, tm=n_seq, name="out_proj_router_sample")

    tb = TB_MOE
    n_blk = (n_all * TOP_K + N_EXPERTS * (tb - 1)) // tb + 1
    cnt = counts[0].astype(I32)
    padded = (cnt + tb - 1) // tb * tb
    pad_end = jnp.cumsum(padded)
    pad_start = pad_end - padded
    slab = jnp.concatenate([slab_p, slab_s], axis=0)
    e_idx = slab[:, :TOP_K].astype(I32)
    rank = slab[:, TOP_K:2 * TOP_K].astype(I32)
    experts = jnp.arange(N_EXPERTS, dtype=I32)
    start_of = jnp.sum(jnp.where(e_idx[..., None] == experts, pad_start, 0), axis=-1)
    dest = (start_of + rank).reshape(-1)
    n_used = (pad_end[-1] // tb).astype(I32).reshape(1)
    blk_start = jnp.arange(n_blk, dtype=I32) * tb
    block_exp = jnp.minimum(jnp.sum((pad_end[None, :] <= blk_start[:, None]).astype(I32), axis=1), N_EXPERTS - 1)

    n_rows = n_blk * tb
    zero_lo = jnp.concatenate([pad_start + cnt, pad_end[-1:]]).astype(I32)
    zero_hi = jnp.concatenate([pad_end, jnp.full((1,), n_rows, I32)]).astype(I32)
    xs = _dispatch(h2_p, h2_s, dest, zero_lo, zero_hi, n_rows)
    fill_end = jnp.sum(jnp.where(block_exp[:, None] == experts, pad_start + cnt, 0), axis=-1)
    rows_filled = jnp.clip(fill_end - blk_start, 0, tb).astype(I32)
    yb = _experts(xs, block_exp, n_used, rows_filled, w_gu[0], b_gu[0], w_down[0], b_down[0])
    y_p = _combine(x1_p, gt2_p, slab_p, dest[:n_p * TOP_K], yb, rows_per_mod=seq, tm=TM_TOK)
    y_s = _combine(x1_s, gt2_s, slab_s, dest[n_p * TOP_K:], yb, rows_per_mod=n_seq, tm=TM_TOK)

    def kv_outputs(dk, dv, ckv, kpe, b, t):
        return (dk.reshape(1, b, t, DIFF_KV_HEADS, 2, DIFF_DH), dv.reshape(1, b, t, DIFF_KV_HEADS, HEAD_DIM),
                ckv.reshape(1, b, t, KV_LORA), kpe.reshape(1, b, t, MLA_ROPE))

    dk_p = jnp.transpose(dk_p.reshape(batch, DIFF_KV_HEADS, 2, DIFF_DH, seq), (0, 4, 1, 2, 3))[None]
    kpe_p = jnp.transpose(kpe_p, (0, 2, 1))[None]
    return ((y_p.reshape(batch, seq, d), y_s.reshape(n_seq, 1, d), dk_p, dv_p.reshape(1, batch, seq, DIFF_KV_HEADS, HEAD_DIM),
             ckv_p.reshape(1, batch, seq, KV_LORA), kpe_p) + kv_outputs(dk_s, dv_s, ckv_s, kpe_s, n_seq, 1))
```

```python
import functools
import math

import jax
import jax.numpy as jnp
from jax import lax
from jax.experimental import pallas as pl
from jax.experimental.pallas import tpu as pltpu

F32 = jnp.float32
BF16 = jnp.bfloat16
U32 = jnp.uint32
I32 = jnp.int32

LANES = 128
V7X_VMEM_BYTES = 64 * 1024 * 1024

HEAD_DIM = 128
DIFF_KV_HEADS = 2
DIFF_GROUP = 4
DIFF_HEADS = DIFF_KV_HEADS * DIFF_GROUP
DIFF_DH = 64
MLA_HEADS = 8
Q_LORA = 512
KV_LORA = 512
MLA_NOPE = 128
MLA_ROPE = 64
MLA_QK_DIM = MLA_NOPE + MLA_ROPE
ROPE_THETA = 10000.0
N_EXPERTS = 32
TOP_K = 4
SWIGLU_LIMIT = 7.0
SWIGLU_ALPHA = 1.702
PAGE_SIZE = 128
RMS_EPS = 1e-6
NEG_INF = -1e30

DIFF_Q_COLS = DIFF_HEADS * 2 * DIFF_DH
DIFF_K_COLS = DIFF_KV_HEADS * 2 * DIFF_DH
DIFF_V_COLS = DIFF_KV_HEADS * HEAD_DIM
O_DQ = 0
O_DK = O_DQ + DIFF_Q_COLS
O_DV = O_DK + DIFF_K_COLS
O_MQ = O_DV + DIFF_V_COLS
O_CKV = O_MQ + Q_LORA
O_KPE = O_CKV + KV_LORA
IN_COLS = O_KPE + MLA_ROPE
IN_COLS_EXT = IN_COLS + MLA_ROPE

TM_PROJ = 256
TQ_DIFF = 256
TQ_MLA = 256
CHAIN_MLA = 128
TK_ATTN = 256
TM_TOK = 128
DMA_UNROLL = 8
TB_MOE = 512
TN_GU = 1024
TN_DOWN = 1024
TN_ADA = 1024
PAGES_PER_CHUNK = 16


def _vmem_limit(nbytes):
    return int(min(max(nbytes * 5 // 4, 16 * 1024 * 1024), V7X_VMEM_BYTES - 4 * 1024 * 1024))


def _cparams(sem, vmem_bytes):
    return pltpu.CompilerParams(dimension_semantics=sem, vmem_limit_bytes=_vmem_limit(vmem_bytes))


def _dot(a, b):
    return jnp.dot(a, b, preferred_element_type=F32)


def _dot_nt(a, b):
    return lax.dot_general(a, b, (((1,), (1,)), ((), ())), preferred_element_type=F32)


def _lane_iota(shape):
    return lax.broadcasted_iota(I32, shape, len(shape) - 1)


def _mod_kernel(c_ref, w_ref, b_ref, o_ref):
    c = c_ref[...]
    a = (c * jax.nn.sigmoid(c)).astype(BF16)
    o_ref[...] = _dot(a, w_ref[...].astype(BF16)) + b_ref[...]


def _modulation(c_all, w_ada, b_ada):
    r, d = c_all.shape
    n = w_ada.shape[1]
    tn = TN_ADA
    return pl.pallas_call(
        _mod_kernel,
        out_shape=jax.ShapeDtypeStruct((r, n), F32),
        grid=(n // tn,),
        in_specs=[pl.BlockSpec((r, d), lambda j: (0, 0)),
                  pl.BlockSpec((d, tn), lambda j: (0, j)),
                  pl.BlockSpec((1, tn), lambda j: (0, j))],
        out_specs=pl.BlockSpec((r, tn), lambda j: (0, j)),
        compiler_params=_cparams(("arbitrary",), 2 * d * tn * 4 + d * tn * 2 + 4 * r * d * 4),
        name="adaln_modulation",
    )(c_all, w_ada, b_ada.reshape(1, n))


def _group64_sumsq(x, gmat):
    x2 = (x * x).astype(BF16)
    outs = [_dot(x2[:, j * LANES:(j + 1) * LANES], gmat) for j in range(x.shape[1] // LANES)]
    return outs[0] if len(outs) == 1 else jnp.concatenate(outs, axis=1)


def _rope_pair_block(blk, tab):
    r = blk * tab
    return r + pltpu.roll(r, 64, 1)


def _proj_kernel(x_ref, sh_ref, sc_ref, tab_ref, g1_ref, win_ref, gmat_ref, gdq_ref, gdk_ref, gqa_ref,
                 gckv_ref, gkpe_ref, wqb_ref, gqn_ref, gqp_ref, wup_ref, *out_refs, sample):
    x = x_ref[...]
    h = x * lax.rsqrt(jnp.mean(x * x, axis=-1, keepdims=True) + RMS_EPS) * g1_ref[...]
    h = h * (1.0 + sc_ref[...]) + sh_ref[...]
    z = _dot(h.astype(BF16), win_ref[...])
    gmat = gmat_ref[...]
    tab = tab_ref[...]
    lane = _lane_iota((1, LANES))
    low_half = lane < 64

    dq = z[:, O_DQ:O_DK]
    qd = dq * lax.rsqrt(_group64_sumsq(dq, gmat) * (1.0 / DIFF_DH) + RMS_EPS) * gdq_ref[...]
    dk = z[:, O_DK:O_DV]
    kd = dk * lax.rsqrt(_group64_sumsq(dk, gmat) * (1.0 / DIFF_DH) + RMS_EPS) * gdk_ref[...]
    vd = z[:, O_DV:O_MQ]
    mq = z[:, O_MQ:O_CKV]
    ckv = z[:, O_CKV:O_KPE]
    kpb = z[:, O_KPE:IN_COLS_EXT]

    ckv_n = ckv * lax.rsqrt(jnp.mean(ckv * ckv, axis=-1, keepdims=True) + RMS_EPS) * gckv_ref[...]
    kp_ms = 0.5 * jnp.sum(kpb * kpb, axis=-1, keepdims=True) * (1.0 / MLA_ROPE)
    kpn = kpb * lax.rsqrt(kp_ms + RMS_EPS) * gkpe_ref[...]
    kr = _rope_pair_block(kpn, tab)

    mqn = (mq * lax.rsqrt(jnp.mean(mq * mq, axis=-1, keepdims=True) + RMS_EPS) * gqa_ref[...]).astype(BF16)
    qm = _dot(mqn, wqb_ref[...])
    gqn = gqn_ref[...]
    gqp = gqp_ref[...]
    q_nope, q_rope = [], []
    for hh in range(MLA_HEADS):
        n_h = qm[:, hh * LANES:(hh + 1) * LANES]
        p_h = qm[:, (MLA_HEADS + hh) * LANES:(MLA_HEADS + hh + 1) * LANES]
        ms = (jnp.sum(n_h * n_h, axis=-1, keepdims=True)
              + 0.5 * jnp.sum(p_h * p_h, axis=-1, keepdims=True)) * (1.0 / MLA_QK_DIM)
        r = lax.rsqrt(ms + RMS_EPS)
        q_nope.append(n_h * r * gqn)
        q_rope.append(_rope_pair_block(p_h * r * gqp, tab))

    if sample:
        qd_ref, dk_ref, dv_ref, ckv_ref, kpe_ref, qlat_ref, qpe_ref = out_refs
        qd_ref[...] = qd
        qlat_ref[...] = jnp.concatenate(
            [_dot(q_nope[hh].astype(BF16), wup_ref[hh]) for hh in range(MLA_HEADS)], axis=1)
        qpe_ref[...] = jnp.concatenate(q_rope, axis=1)
    else:
        qds_ref, dk_ref, dv_ref, ckv_ref, kpe_ref, qmla_ref, kmla_ref, vmla_ref = out_refs
        blocks = []
        for c in range(DIFF_HEADS):
            blk = qd[:, c * LANES:(c + 1) * LANES]
            blocks.append(jnp.where(low_half, blk, 0.0))
            blocks.append(jnp.where(low_half, 0.0, blk))
        qds_ref[...] = jnp.concatenate(blocks, axis=1).astype(BF16)
        kvup = _dot(ckv_n.astype(BF16), wup_ref[...])
        kr_lo = jnp.where(low_half, kr, 0.0)
        qb, kb = [], []
        for hh in range(MLA_HEADS):
            qb += [q_nope[hh], q_rope[hh]]
            kb += [kvup[:, hh * LANES:(hh + 1) * LANES], kr_lo]
        qmla_ref[...] = jnp.concatenate(qb, axis=1).astype(BF16)
        kmla_ref[...] = jnp.concatenate(kb, axis=1).astype(BF16)
        vmla_ref[...] = kvup[:, MLA_HEADS * LANES:].astype(BF16)
    if sample:
        dk_ref[...] = kd
        kpe_ref[...] = kr[:, :MLA_ROPE]
    else:
        dk_ref[...] = kd.T
        kpe_ref[...] = kr.T[:MLA_ROPE, :]
    dv_ref[...] = vd
    ckv_ref[...] = ckv_n


def _project(x2d, shift, scale, tab, consts, wup, *, sample, rows_per_mod, tm):
    n, d = x2d.shape
    steps = n // tm
    mod_rows = shift.shape[1]
    tab_rows = tab.shape[0]
    tab_block = tm if tab_rows > 1 else 1
    tab_steps = max(tab_rows // tm, 1)
    mod_spec = pl.BlockSpec((None, mod_rows, d), lambda i: (i * tm // rows_per_mod, 0, 0))
    tab_spec = pl.BlockSpec((tab_block, LANES), lambda i: (i % tab_steps, 0))

    def full(a):
        nd = a.ndim
        return pl.BlockSpec(a.shape, lambda i: (0,) * nd)

    def rows(cols, dt):
        return jax.ShapeDtypeStruct((n, cols), dt), pl.BlockSpec((tm, cols), lambda i: (i, 0))

    if sample:
        outs = [rows(DIFF_Q_COLS, F32), rows(DIFF_K_COLS, F32), rows(DIFF_V_COLS, F32), rows(KV_LORA, F32),
                rows(MLA_ROPE, F32), rows(MLA_HEADS * KV_LORA, F32), rows(MLA_HEADS * LANES, F32)]
    else:
        per_seq = rows_per_mod // tm

        def cols_t(width):
            return (jax.ShapeDtypeStruct((n // rows_per_mod, width, rows_per_mod), F32),
                    pl.BlockSpec((None, width, tm), lambda i: (i // per_seq, 0, i % per_seq)))

        outs = [rows(2 * DIFF_Q_COLS, BF16), cols_t(DIFF_K_COLS), rows(DIFF_V_COLS, F32), rows(KV_LORA, F32),
                cols_t(MLA_ROPE), rows(2 * MLA_HEADS * LANES, BF16), rows(2 * MLA_HEADS * LANES, BF16),
                rows(MLA_HEADS * LANES, BF16)]
    out_shape = [o[0] for o in outs]
    out_specs = [o[1] for o in outs]
    in_specs = [pl.BlockSpec((tm, d), lambda i: (i, 0)), mod_spec, mod_spec, tab_spec] + [full(a) for a in consts] + [full(wup)]
    weight_bytes = sum(int(a.size) * a.dtype.itemsize for a in consts) + int(wup.size) * wup.dtype.itemsize
    vmem = 2 * weight_bytes + 2 * tm * d * 4 + 12 * tm * IN_COLS_EXT * 4
    return pl.pallas_call(
        functools.partial(_proj_kernel, sample=sample),
        out_shape=out_shape, grid=(steps,), in_specs=in_specs, out_specs=out_specs,
        compiler_params=_cparams(("arbitrary",), vmem),
        name="in_proj_sample" if sample else "in_proj_prompt",
    )(x2d, shift, scale, tab, *consts, wup)


def _softmax_step(s, v_bf, m, l, acc):
    m_new = jnp.maximum(m, jnp.max(s, axis=-1, keepdims=True))
    alpha = jnp.exp(m - m_new)
    p = jnp.exp(s - m_new)
    l_new = alpha * l + jnp.sum(p, axis=-1, keepdims=True)
    acc_new = alpha * acc + _dot(p.astype(BF16), v_bf)
    return m_new, l_new, acc_new


def _diff_lambda(lv, lam_init):
    a = jnp.sum(lv[0:1, :] * lv[1:2, :], axis=-1, keepdims=True)
    b = jnp.sum(lv[2:3, :] * lv[3:4, :], axis=-1, keepdims=True)
    return jnp.exp(a) - jnp.exp(b) + lam_init


def _causal_blocks(i, tq, tk):
    assert tq % tk == 0 or tk % tq == 0
    return (i * tq) // tk, max(tq // tk, 1)


def _chain_stats(s, m_old, l_old):
    m_new = jnp.maximum(m_old, jnp.max(s, axis=-1, keepdims=True))
    alpha = jnp.exp(m_old - m_new)
    p = jnp.exp(s - m_new)
    return m_new, alpha * l_old + jnp.sum(p, axis=-1, keepdims=True), p.astype(BF16), alpha


def _chain_init(n_chains, rows):
    return tuple((jnp.full((rows, 1), NEG_INF, F32), jnp.zeros((rows, 1), F32), jnp.zeros((rows, HEAD_DIM), F32))
                 for _ in range(n_chains))


def _diff_prompt_kernel(lam_ref, gsub_ref, q_ref, kt_ref, v_ref, o_ref, *, tq, tk, lam_init):
    kvh = pl.program_id(1)
    i = pl.program_id(2)
    q = q_ref[...]
    nrow = 2 * DIFF_GROUP
    qs = jnp.concatenate([q[:, c * LANES:(c + 1) * LANES] for c in range(nrow)], axis=0)
    kv_scale = jnp.where(kvh == 0, 1.0, 2.0 ** (-DIFF_GROUP)).astype(F32)
    rel = (lax.broadcasted_iota(I32, (tq, tk), 0) - lax.broadcasted_iota(I32, (tq, tk), 1))

    def step(j, carry, masked):
        m, l, acc = carry
        start = pl.multiple_of(j * tk, tk)
        kt = kt_ref[:, pl.ds(start, tk)].astype(BF16)
        v = v_ref[pl.ds(start, tk), :].astype(BF16)
        s = _dot(qs, kt)
        dist = rel + (i * tq - j * tk)
        distf = dist.astype(F32)
        pieces = []
        for g in range(DIFF_GROUP):
            bias = (2.0 ** (-(g + 1))) * kv_scale * distf
            for mp in range(2):
                r0 = (g * 2 + mp) * tq
                sg = s[r0:r0 + tq] - bias
                if masked:
                    sg = jnp.where(dist >= 0, sg, NEG_INF)
                pieces.append(sg)
        s = jnp.concatenate(pieces, axis=0)
        return _softmax_step(s, v, m, l, acc)

    init = (jnp.full((nrow * tq, 1), NEG_INF, F32), jnp.zeros((nrow * tq, 1), F32),
            jnp.zeros((nrow * tq, HEAD_DIM), F32))
    n_full, n_diag = _causal_blocks(i, tq, tk)
    carry = lax.fori_loop(0, n_full, lambda j, c: step(j, c, False), init)
    for jj in range(n_diag):
        carry = step(n_full + jj, carry, True)
    m, l, acc = carry
    o = acc / l
    lam = _diff_lambda(lam_ref[...], lam_init)
    gsub = gsub_ref[...] * (1.0 - lam_init)
    outs = []
    for g in range(DIFF_GROUP):
        og = o[(2 * g) * tq:(2 * g + 1) * tq] - lam * o[(2 * g + 1) * tq:(2 * g + 2) * tq]
        og = og * lax.rsqrt(jnp.mean(og * og, axis=-1, keepdims=True) + RMS_EPS) * gsub
        outs.append(og)
    o_ref[...] = jnp.concatenate(outs, axis=1).astype(o_ref.dtype)


def _diff_prompt_attention(qds, dk_t, dv, lam_v, gsub, *, batch, seq, lam_init):
    tq, tk = TQ_DIFF, TK_ATTN
    nq = seq // tq
    nrow = 2 * DIFF_GROUP
    width = nrow * LANES
    vmem = 2 * (tq * width * 2 + 2 * seq * LANES * 4) + 6 * nrow * tq * tk * 4 + 3 * nrow * tq * LANES * 4
    return pl.pallas_call(
        functools.partial(_diff_prompt_kernel, tq=tq, tk=tk, lam_init=lam_init),
        out_shape=jax.ShapeDtypeStruct((batch * seq, DIFF_HEADS * HEAD_DIM), BF16),
        grid=(batch, DIFF_KV_HEADS, nq),
        in_specs=[pl.BlockSpec(lam_v.shape, lambda b, h, i: (0, 0)),
                  pl.BlockSpec(gsub.shape, lambda b, h, i: (0, 0)),
                  pl.BlockSpec((tq, width), lambda b, h, i: (b * nq + i, h)),
                  pl.BlockSpec((None, LANES, seq), lambda b, h, i: (b, h, 0)),
                  pl.BlockSpec((seq, HEAD_DIM), lambda b, h, i: (b, h))],
        out_specs=pl.BlockSpec((tq, DIFF_GROUP * HEAD_DIM), lambda b, h, i: (b * nq + i, h)),
        compiler_params=_cparams(("arbitrary", "arbitrary", "arbitrary"), vmem),
        name="diff_attn_prompt",
    )(lam_v, gsub, qds, dk_t, dv)


def _mla_prompt_kernel(q_ref, k_ref, v_ref, o_ref, *, tq, tk, chain):
    i = pl.program_id(1)
    qk = 2 * LANES
    per_head = tq // chain
    rel = (lax.broadcasted_iota(I32, (chain, tk), 0) - lax.broadcasted_iota(I32, (chain, tk), 1))

    def step(j, state, masked):
        start = pl.multiple_of(j * tk, tk)
        scores = []
        for hh in range(MLA_HEADS):
            k = k_ref[pl.ds(start, tk), hh * qk:(hh + 1) * qk]
            for r in range(per_head):
                scores.append(_dot_nt(q_ref[r * chain:(r + 1) * chain, hh * qk:(hh + 1) * qk], k))
        stats = []
        for hh in range(MLA_HEADS):
            for r in range(per_head):
                c = hh * per_head + r
                s = scores[c]
                if masked:
                    s = jnp.where(rel + (i * tq + r * chain - j * tk) >= 0, s, NEG_INF)
                stats.append(_chain_stats(s, state[c][0], state[c][1]))
        new = []
        for hh in range(MLA_HEADS):
            v = v_ref[pl.ds(start, tk), hh * HEAD_DIM:(hh + 1) * HEAD_DIM]
            for r in range(per_head):
                c = hh * per_head + r
                m_new, l_new, p, alpha = stats[c]
                new.append((m_new, l_new, alpha * state[c][2] + _dot(p, v)))
        return tuple(new)

    n_full, n_diag = _causal_blocks(i, tq, tk)
    state = lax.fori_loop(0, n_full, lambda j, st: step(j, st, False), _chain_init(MLA_HEADS * per_head, chain))
    for jj in range(n_diag):
        state = step(n_full + jj, state, True)
    for hh in range(MLA_HEADS):
        o_h = jnp.concatenate([state[hh * per_head + r][2] / state[hh * per_head + r][1] for r in range(per_head)],
                              axis=0)
        o_ref[:, hh * HEAD_DIM:(hh + 1) * HEAD_DIM] = o_h.astype(o_ref.dtype)


def _mla_prompt_attention(qmla, kmla, vmla, *, batch, seq):
    tq, tk, chain = TQ_MLA, TK_ATTN, CHAIN_MLA
    nq = seq // tq
    qk = 2 * LANES
    vmem = (2 * (tq * MLA_HEADS * qk * 2 + seq * MLA_HEADS * qk * 2 + seq * MLA_HEADS * HEAD_DIM * 2
                 + tq * MLA_HEADS * HEAD_DIM * 2) + 3 * MLA_HEADS * tq * LANES * 4 + 16 * chain * tk * 4)
    return pl.pallas_call(
        functools.partial(_mla_prompt_kernel, tq=tq, tk=tk, chain=chain),
        out_shape=jax.ShapeDtypeStruct((batch * seq, MLA_HEADS * HEAD_DIM), BF16),
        grid=(batch, nq),
        in_specs=[pl.BlockSpec((tq, MLA_HEADS * qk), lambda b, i: (b * nq + i, 0)),
                  pl.BlockSpec((seq, MLA_HEADS * qk), lambda b, i: (b, 0)),
                  pl.BlockSpec((seq, MLA_HEADS * HEAD_DIM), lambda b, i: (b, 0))],
        out_specs=pl.BlockSpec((tq, MLA_HEADS * HEAD_DIM), lambda b, i: (b * nq + i, 0)),
        compiler_params=_cparams(("arbitrary", "arbitrary"), vmem),
        name="mla_attn_prompt",
    )(qmla, kmla, vmla)


def _decode_kernel(pt_ref, lam_ref, gsub_ref, slope_ref, qlat_ref, qpe_ref, ckvn_ref, kpen_ref, qdm_ref,
                   kdn_ref, vdn_ref, c_ckv, c_kpe, c_dk, c_dv, olat_ref, od_ref,
                   ckv_buf, kpe_buf, dk_buf, dv_buf, sem, *, n_seq, n_chunks, pages, lam_init):
    s_idx = pl.program_id(0)
    chunk_len = pages * PAGE_SIZE

    def copies(seq, chunk, slot):
        out = []
        for p in range(pages):
            page = pt_ref[seq, chunk * pages + p]
            out.append(pltpu.make_async_copy(
                c_ckv.at[page], ckv_buf.at[slot, pl.ds(p * PAGE_SIZE, PAGE_SIZE), :], sem.at[slot]))
            out.append(pltpu.make_async_copy(
                c_kpe.at[page], kpe_buf.at[slot, :, pl.ds(p * PAGE_SIZE, PAGE_SIZE)], sem.at[slot]))
            out.append(pltpu.make_async_copy(
                c_dk.at[page], dk_buf.at[slot, :, pl.ds(p * PAGE_SIZE, PAGE_SIZE)], sem.at[slot]))
            out.append(pltpu.make_async_copy(
                c_dv.at[page], dv_buf.at[slot, pl.ds(2 * p * PAGE_SIZE, 2 * PAGE_SIZE), :], sem.at[slot]))
        return out

    def start_chunk(seq, chunk, slot):
        for n, cp in enumerate(copies(seq, chunk, slot)):
            cp.start(priority=min(n % 4, 1))

    def wait_chunk(seq, chunk, slot):
        for cp in copies(seq, chunk, slot):
            cp.wait()

    @pl.when(s_idx == 0)
    def _():
        start_chunk(0, 0, 0)

    qlat = qlat_ref[...]
    qpe = qpe_ref[...]
    qdm = qdm_ref[...]
    qlat_bf, qpe_bf, qdm_bf = qlat.astype(BF16), qpe.astype(BF16), qdm.astype(BF16)
    ckv_new = ckvn_ref[...]
    slopes = slope_ref[...][:, 0:1]
    past_len = n_chunks * chunk_len

    m_m = (jnp.sum(qlat * ckv_new, axis=-1, keepdims=True)
           + jnp.sum(qpe * kpen_ref[...], axis=-1, keepdims=True))
    l_m = jnp.ones_like(m_m)
    acc_m = jnp.broadcast_to(ckv_new, (MLA_HEADS, KV_LORA)).astype(F32)
    m_d = jnp.sum(qdm * kdn_ref[...], axis=-1, keepdims=True)
    l_d = jnp.ones_like(m_d)
    vdn = vdn_ref[...]
    half = 2 * DIFF_GROUP
    acc_d = jnp.concatenate([jnp.broadcast_to(vdn[:, :HEAD_DIM], (half, HEAD_DIM)),
                             jnp.broadcast_to(vdn[:, HEAD_DIM:], (half, HEAD_DIM))], axis=0).astype(F32)

    def body(c, carry):
        m_m, l_m, acc_m, m_d, l_d, acc_d = carry
        step = s_idx * n_chunks + c
        slot = step % 2

        @pl.when(c + 1 < n_chunks)
        def _():
            start_chunk(s_idx, c + 1, 1 - slot)

        @pl.when(jnp.logical_and(c + 1 == n_chunks, s_idx + 1 < n_seq))
        def _():
            start_chunk(s_idx + 1, 0, 1 - slot)

        wait_chunk(s_idx, c, slot)

        ckv = ckv_buf[slot].astype(BF16)
        s_m = _dot_nt(qlat_bf, ckv) + _dot(qpe_bf, kpe_buf[slot].astype(BF16))
        m_m, l_m, acc_m = _softmax_step(s_m, ckv, m_m, l_m, acc_m)

        s_d = _dot(qdm_bf, dk_buf[slot].astype(BF16))
        pos = c * chunk_len + _lane_iota((1, chunk_len))
        s_d = s_d - slopes * (past_len - pos).astype(F32)
        m_new = jnp.maximum(m_d, jnp.max(s_d, axis=-1, keepdims=True))
        alpha = jnp.exp(m_d - m_new)
        p = jnp.exp(s_d - m_new)
        l_d = alpha * l_d + jnp.sum(p, axis=-1, keepdims=True)
        p_bf = p.astype(BF16)
        v0 = dv_buf[slot, pl.ds(0, chunk_len, stride=2), :].astype(BF16)
        v1 = dv_buf[slot, pl.ds(1, chunk_len, stride=2), :].astype(BF16)
        pv = jnp.concatenate([_dot(p_bf[:half], v0), _dot(p_bf[half:], v1)], axis=0)
        acc_d = alpha * acc_d + pv
        return m_m, l_m, acc_m, m_new, l_d, acc_d

    m_m, l_m, acc_m, m_d, l_d, acc_d = lax.fori_loop(0, n_chunks, body, (m_m, l_m, acc_m, m_d, l_d, acc_d))

    olat_ref[...] = acc_m / l_m
    o = acc_d / l_d
    lam = _diff_lambda(lam_ref[...], lam_init)
    g4 = DIFF_GROUP
    od = jnp.concatenate([o[0:g4] - lam * o[g4:2 * g4], o[2 * g4:3 * g4] - lam * o[3 * g4:4 * g4]], axis=0)
    od = od * lax.rsqrt(jnp.mean(od * od, axis=-1, keepdims=True) + RMS_EPS) * (gsub_ref[...] * (1.0 - lam_init))
    od_ref[...] = od


def _decode_attention(page_table, lam_v, gsub, slopes, qlat, qpe, ckv_new, kpe_new, qdm, kd_new, vd_new,
                      c_ckv, c_kpe, c_dk, c_dv, *, lam_init):
    n_seq, n_pages = page_table.shape
    pages = min(PAGES_PER_CHUNK, n_pages)
    n_chunks = n_pages // pages
    chunk_len = pages * PAGE_SIZE

    def per_seq(a):
        return pl.BlockSpec((None,) + a.shape[1:], lambda s, pt: (s,) + (0,) * (a.ndim - 1))

    def full(a):
        return pl.BlockSpec(a.shape, lambda s, pt: (0,) * a.ndim)

    any_spec = pl.BlockSpec(memory_space=pl.ANY)
    buf_bytes = 2 * chunk_len * (KV_LORA + MLA_ROPE + DIFF_K_COLS + DIFF_V_COLS) * 4
    grid_spec = pltpu.PrefetchScalarGridSpec(
        num_scalar_prefetch=1,
        grid=(n_seq,),
        in_specs=[full(lam_v), full(gsub), full(slopes), per_seq(qlat), per_seq(qpe), per_seq(ckv_new),
                  per_seq(kpe_new), per_seq(qdm), per_seq(kd_new), per_seq(vd_new),
                  any_spec, any_spec, any_spec, any_spec],
        out_specs=[pl.BlockSpec((None, MLA_HEADS, KV_LORA), lambda s, pt: (s, 0, 0)),
                   pl.BlockSpec((None, DIFF_HEADS, HEAD_DIM), lambda s, pt: (s, 0, 0))],
        scratch_shapes=[pltpu.VMEM((2, chunk_len, KV_LORA), F32),
                        pltpu.VMEM((2, MLA_ROPE, chunk_len), F32),
                        pltpu.VMEM((2, DIFF_K_COLS, chunk_len), F32),
                        pltpu.VMEM((2, 2 * chunk_len, HEAD_DIM), F32),
                        pltpu.SemaphoreType.DMA((2,))],
    )
    return pl.pallas_call(
        functools.partial(_decode_kernel, n_seq=n_seq, n_chunks=n_chunks, pages=pages, lam_init=lam_init),
        out_shape=[jax.ShapeDtypeStruct((n_seq, MLA_HEADS, KV_LORA), F32),
                   jax.ShapeDtypeStruct((n_seq, DIFF_HEADS, HEAD_DIM), F32)],
        grid_spec=grid_spec,
        compiler_params=_cparams(("arbitrary",), buf_bytes + buf_bytes // 2),
        name="decode_attn",
    )(page_table, lam_v, gsub, slopes, qlat, qpe, ckv_new, kpe_new, qdm, kd_new, vd_new, c_ckv, c_kpe, c_dk, c_dv)


def _latent_out_kernel(o_ref, w_ref, out_ref):
    out_ref[...] = _dot(o_ref[...].astype(BF16), w_ref[...]).astype(out_ref.dtype)


def _latent_out(olat2d, w_uv_bf):
    n = olat2d.shape[0]
    return pl.pallas_call(
        _latent_out_kernel,
        out_shape=jax.ShapeDtypeStruct((n, MLA_HEADS * HEAD_DIM), BF16),
        grid=(MLA_HEADS,),
        in_specs=[pl.BlockSpec((n, KV_LORA), lambda h: (0, h)),
                  pl.BlockSpec((None, KV_LORA, HEAD_DIM), lambda h: (h, 0, 0))],
        out_specs=pl.BlockSpec((n, HEAD_DIM), lambda h: (0, h)),
        compiler_params=_cparams(("arbitrary",), 4 * n * KV_LORA * 4),
        name="latent_out",
    )(olat2d, w_uv_bf)


def _pack_bf16_pairs(a, b):
    ua = pltpu.bitcast(a.astype(BF16).astype(F32), U32) & jnp.uint32(0xFFFF0000)
    ub = pltpu.bitcast(b.astype(BF16).astype(F32), U32) >> jnp.uint32(16)
    return ua | ub


def _unpack_bf16_pairs(u):
    a = pltpu.bitcast(u & jnp.uint32(0xFFFF0000), F32)
    b = pltpu.bitcast(u << jnp.uint32(16), F32)
    return a, b


def _outproj_kernel(ad_ref, am_ref, x_ref, gt_ref, sh_ref, sc_ref, g2_ref, wo_ref, wr_ref, br_ref, tri_ref, cin_ref,
                    x1_ref, h2_ref, slab_ref, cout_ref, carry_ref):
    i = pl.program_id(0)

    @pl.when(i == 0)
    def _():
        carry_ref[...] = cin_ref[...]

    half = wo_ref.shape[0] // 2
    mix = _dot(ad_ref[...], wo_ref[:half, :]) + _dot(am_ref[...], wo_ref[half:, :])
    x1 = x_ref[...] + gt_ref[...] * mix
    x1_ref[...] = x1
    h = x1 * lax.rsqrt(jnp.mean(x1 * x1, axis=-1, keepdims=True) + RMS_EPS) * g2_ref[...]
    h = h * (1.0 + sc_ref[...]) + sh_ref[...]
    d = h.shape[1]
    h2_ref[...] = _pack_bf16_pairs(h[:, :d // 2], h[:, d // 2:])

    h_hi = h.astype(BF16)
    h_lo = (h - h_hi.astype(F32)).astype(BF16)
    r2 = _dot(jnp.concatenate([h_hi, h_lo], axis=1), wr_ref[...])
    ne = r2.shape[1] // 2
    logits = r2[:, :ne] + r2[:, ne:] + br_ref[...]
    tm = logits.shape[0]
    lane = _lane_iota((tm, ne)).astype(F32)
    slab_lane = _lane_iota((tm, LANES))
    carry = carry_ref[...]
    work = logits
    onehots, vals, idxs = [], [], []
    for _ in range(TOP_K):
        mx = jnp.max(work, axis=-1, keepdims=True)
        ix = jnp.min(jnp.where(work == mx, lane, float(ne)), axis=-1, keepdims=True)
        oh = lane == ix
        onehots.append(oh)
        vals.append(mx)
        idxs.append(ix)
        work = jnp.where(oh, -jnp.inf, work)
    cnt = sum(jnp.where(oh, 1.0, 0.0) for oh in onehots)
    before = _dot(tri_ref[...], cnt.astype(BF16)) + carry
    exps = [jnp.exp(v - vals[0]) for v in vals]
    den = sum(exps)
    slab = jnp.zeros((tm, LANES), F32)
    for k in range(TOP_K):
        rank = jnp.sum(jnp.where(onehots[k], before, 0.0), axis=-1, keepdims=True)
        slab = jnp.where(slab_lane == k, idxs[k], slab)
        slab = jnp.where(slab_lane == TOP_K + k, rank, slab)
        slab = jnp.where(slab_lane == 2 * TOP_K + k, exps[k] / den, slab)
    slab_ref[...] = slab
    new_carry = carry + jnp.sum(cnt, axis=0, keepdims=True)
    carry_ref[...] = new_carry
    cout_ref[...] = new_carry


def _out_project(attn_d, attn_m, x2d, gate, shift, scale, consts, count_in, *, rows_per_mod, tm, name):
    n, d = x2d.shape
    steps = n // tm
    mod_rows = gate.shape[1]
    mod_spec = pl.BlockSpec((None, mod_rows, d), lambda i: (i * tm // rows_per_mod, 0, 0))

    def full(a):
        nd = a.ndim
        return pl.BlockSpec(a.shape, lambda i: (0,) * nd)

    in_specs = [pl.BlockSpec((tm, d // 2), lambda i: (i, 0)), pl.BlockSpec((tm, d // 2), lambda i: (i, 0)),
                pl.BlockSpec((tm, d), lambda i: (i, 0)), mod_spec, mod_spec, mod_spec] + [full(a) for a in consts] + [full(count_in)]
    out_shape = [jax.ShapeDtypeStruct((n, d), F32), jax.ShapeDtypeStruct((n, d // 2), U32),
                 jax.ShapeDtypeStruct((n, LANES), F32), jax.ShapeDtypeStruct(count_in.shape, F32)]
    out_specs = [pl.BlockSpec((tm, d), lambda i: (i, 0)), pl.BlockSpec((tm, d // 2), lambda i: (i, 0)),
                 pl.BlockSpec((tm, LANES), lambda i: (i, 0)), full(count_in)]
    weight_bytes = sum(int(a.size) * a.dtype.itemsize for a in consts)
    return pl.pallas_call(
        _outproj_kernel, out_shape=out_shape, grid=(steps,), in_specs=in_specs, out_specs=out_specs,
        scratch_shapes=[pltpu.VMEM(count_in.shape, F32)],
        compiler_params=_cparams(("arbitrary",), 2 * weight_bytes + 16 * tm * d * 4),
        name=name,
    )(attn_d, attn_m, x2d, gate, shift, scale, *consts, count_in)


ZERO_CHUNK = 128


def _zero_row_ranges(lo_ref, hi_ref, zbuf, xs_out, sem, n_ranges):
    def pieces(g):
        lo, hi = lo_ref[g], hi_ref[g]
        n1 = (-lo) & 7
        a8 = lo + n1
        n8 = ((-a8) & (ZERO_CHUNK - 1)) >> 3
        a_big = a8 + 8 * n8
        n_big = (hi - a_big) // ZERO_CHUNK
        return ((n1, 1, lambda t: lo + t),
                (n8, 8, lambda t: pl.multiple_of(a8 + 8 * t, 8)),
                (n_big, ZERO_CHUNK, lambda t: pl.multiple_of(a_big + ZERO_CHUNK * t, ZERO_CHUNK)))

    def for_all(action):
        def per_range(g, c):
            for count, rows, row_of in pieces(g):
                def one(t, cc, rows=rows, row_of=row_of):
                    action(pltpu.make_async_copy(zbuf.at[pl.ds(0, rows), :], xs_out.at[pl.ds(row_of(t), rows), :], sem))
                    return cc
                lax.fori_loop(0, count, one, 0)
            return c
        lax.fori_loop(0, n_ranges, per_range, 0)

    for_all(lambda cp: cp.start())
    for_all(lambda cp: cp.wait())


def _dispatch_kernel(zlo_ref, zhi_ref, dest_ref, hp_ref, hs_ref, xs_out, zbuf, sem, *, tm, prompt_steps, n_ranges):
    @pl.when(pl.program_id(0) == 0)
    def _():
        zbuf[...] = jnp.zeros(zbuf.shape, zbuf.dtype)
        _zero_row_ranges(zlo_ref, zhi_ref, zbuf, xs_out, sem, n_ranges)

    def scatter_rows(h_ref):
        def row_copy(r, k):
            return pltpu.make_async_copy(h_ref.at[pl.ds(r, 1), :],
                                         xs_out.at[pl.ds(dest_ref[r * TOP_K + k], 1), :], sem)

        def issue(r, c):
            for k in range(TOP_K):
                row_copy(r, k).start(priority=k % 2)
            return c

        def drain(r, c):
            for k in range(TOP_K):
                row_copy(r, k).wait()
            return c

        lax.fori_loop(0, tm, issue, 0, unroll=DMA_UNROLL)
        lax.fori_loop(0, tm, drain, 0, unroll=DMA_UNROLL)

    @pl.when(pl.program_id(0) < prompt_steps)
    def _():
        scatter_rows(hp_ref)

    @pl.when(pl.program_id(0) >= prompt_steps)
    def _():
        scatter_rows(hs_ref)


def _dispatch(h2_p, h2_s, dest_flat, zero_lo, zero_hi, n_rows):
    tm = TM_TOK
    w = h2_p.shape[1]
    p_steps, s_steps = h2_p.shape[0] // tm, h2_s.shape[0] // tm
    grid_spec = pltpu.PrefetchScalarGridSpec(
        num_scalar_prefetch=2,
        grid=(p_steps + s_steps,),
        in_specs=[pl.BlockSpec((tm * TOP_K,), lambda i, zl, zh: (i,), memory_space=pltpu.SMEM),
                  pl.BlockSpec((tm, w), lambda i, zl, zh: (jnp.minimum(i, p_steps - 1), 0)),
                  pl.BlockSpec((tm, w), lambda i, zl, zh: (jnp.maximum(i - p_steps, 0), 0))],
        out_specs=pl.BlockSpec(memory_space=pl.ANY),
        scratch_shapes=[pltpu.VMEM((ZERO_CHUNK, w), U32), pltpu.SemaphoreType.DMA(())])
    return pl.pallas_call(
        functools.partial(_dispatch_kernel, tm=tm, prompt_steps=p_steps, n_ranges=zero_lo.shape[0]),
        out_shape=jax.ShapeDtypeStruct((n_rows, w), U32),
        grid_spec=grid_spec,
        compiler_params=pltpu.CompilerParams(dimension_semantics=("arbitrary",), has_side_effects=True),
        name="moe_dispatch",
    )(zero_lo, zero_hi, dest_flat, h2_p, h2_s)


def _expert_changed(be_ref, i):
    prev = be_ref[jnp.maximum(i - 1, 0)]
    return jnp.logical_or(i == 0, be_ref[i] != prev)


def _by_filled_half(i, nu_ref, rv_ref, out_ref, compute):
    tb = out_ref.shape[0]
    half = tb // 2
    active = i < nu_ref[0]
    filled = rv_ref[jnp.minimum(i, nu_ref[0] - 1)]

    @pl.when(jnp.logical_and(active, filled > half))
    def _():
        compute(slice(0, tb))

    @pl.when(jnp.logical_and(active, filled <= half))
    def _():
        compute(slice(0, half))
        out_ref[half:, :] = jnp.zeros((tb - half, out_ref.shape[1]), out_ref.dtype)

    @pl.when(jnp.logical_not(active))
    def _():
        out_ref[...] = jnp.zeros_like(out_ref)


def _load_expert_weights(be_ref, nu_ref, ri_ref, nx_ref, meta_ref, w_hbm, wbuf, sem, wa_bf, wb_bf, *, second_off):
    j, i = pl.program_id(0), pl.program_id(1)
    n_pass = pl.num_programs(0)
    tile = wa_bf.shape[1]
    n_runs, first_e = meta_ref[0], meta_ref[1]

    def fetch(e, jj, slot):
        lo = pl.multiple_of(jj * tile, tile)
        hi = pl.multiple_of((jj + second_off) * tile, tile)
        return (pltpu.make_async_copy(w_hbm.at[e, :, pl.ds(lo, tile)], wbuf.at[slot, 0], sem.at[slot]),
                pltpu.make_async_copy(w_hbm.at[e, :, pl.ds(hi, tile)], wbuf.at[slot, 1], sem.at[slot]))

    @pl.when(jnp.logical_and(i < nu_ref[0], _expert_changed(be_ref, i)))
    def _():
        run = j * n_runs + ri_ref[i]
        slot = run % 2
        e = be_ref[i]

        @pl.when(run == 0)
        def _():
            for cp in fetch(e, j, slot):
                cp.start()

        for cp in fetch(e, j, slot):
            cp.wait()
        wa_bf[...] = wbuf[slot, 0].astype(BF16)
        wb_bf[...] = wbuf[slot, 1].astype(BF16)
        nxt = nx_ref[i]

        @pl.when(nxt < N_EXPERTS)
        def _():
            for cp in fetch(nxt, j, 1 - slot):
                cp.start()

        @pl.when(jnp.logical_and(nxt >= N_EXPERTS, j + 1 < n_pass))
        def _():
            for cp in fetch(first_e, j + 1, 1 - slot):
                cp.start()


def _gate_up_kernel(be_ref, nu_ref, rv_ref, ri_ref, nx_ref, meta_ref, xs_ref, w_hbm, bg_ref, bu_ref, act_ref,
                    wg_bf, wu_bf, wbuf, sem, *, second_off):
    i = pl.program_id(1)
    _load_expert_weights(be_ref, nu_ref, ri_ref, nx_ref, meta_ref, w_hbm, wbuf, sem, wg_bf, wu_bf,
                         second_off=second_off)

    def compute(rows):
        a, b = _unpack_bf16_pairs(xs_ref[rows, :])
        x = jnp.concatenate([a, b], axis=1).astype(BF16)
        gate = jnp.minimum(_dot(x, wg_bf[...]) + bg_ref[...], SWIGLU_LIMIT)
        up = jnp.clip(_dot(x, wu_bf[...]) + bu_ref[...], -SWIGLU_LIMIT, SWIGLU_LIMIT)
        act_ref[rows, :] = ((up + 1.0) * gate * jax.nn.sigmoid(SWIGLU_ALPHA * gate)).astype(act_ref.dtype)

    _by_filled_half(i, nu_ref, rv_ref, act_ref, compute)


def _down_kernel(be_ref, nu_ref, rv_ref, ri_ref, nx_ref, meta_ref, act_ref, w_hbm, ba_ref, bb_ref, y_ref,
                 wa_bf, wb_bf, wbuf, sem, *, second_off):
    i = pl.program_id(1)
    _load_expert_weights(be_ref, nu_ref, ri_ref, nx_ref, meta_ref, w_hbm, wbuf, sem, wa_bf, wb_bf,
                         second_off=second_off)

    def compute(rows):
        act = act_ref[rows, :]
        ya = _dot(act, wa_bf[...]) + ba_ref[...]
        yb = _dot(act, wb_bf[...]) + bb_ref[...]
        y_ref[rows, :] = _pack_bf16_pairs(ya, yb)

    _by_filled_half(i, nu_ref, rv_ref, y_ref, compute)


def _experts(xs, sched, w_gu, b_gu, w_down, b_down):
    n_rows, half_d = xs.shape
    d = 2 * half_d
    ne, _, two_f = w_gu.shape
    f = two_f // 2
    tb, tn, tn2 = TB_MOE, TN_GU, TN_DOWN
    n_blk = n_rows // tb
    b_gu3 = b_gu.reshape(ne, 1, two_f)
    b_down3 = b_down.reshape(ne, 1, d)

    def grid_spec(in_cols, k_dim, tile, second_off):
        def blk(i, nu):
            return jnp.minimum(i, nu[0] - 1)

        def b_spec(off):
            return pl.BlockSpec((None, 1, tile), lambda j, i, be, nu, *_: (be[blk(i, nu)], 0, j + off))

        return pltpu.PrefetchScalarGridSpec(
            num_scalar_prefetch=len(sched), grid=(second_off, n_blk),
            in_specs=[pl.BlockSpec((tb, in_cols), lambda j, i, be, nu, *_: (blk(i, nu), 0)),
                      pl.BlockSpec(memory_space=pl.ANY), b_spec(0), b_spec(second_off)],
            out_specs=pl.BlockSpec((tb, tile), lambda j, i, *_: (i, j)),
            scratch_shapes=[pltpu.VMEM((k_dim, tile), BF16), pltpu.VMEM((k_dim, tile), BF16),
                            pltpu.VMEM((2, 2, k_dim, tile), F32), pltpu.SemaphoreType.DMA((2,))])

    act = pl.pallas_call(
        functools.partial(_gate_up_kernel, second_off=f // tn), out_shape=jax.ShapeDtypeStruct((n_rows, f), BF16),
        grid_spec=grid_spec(half_d, d, tn, f // tn),
        compiler_params=_cparams(("arbitrary", "arbitrary"), 4 * d * tn * 4 + 2 * d * tn * 2 + 8 * tb * d * 4),
        name="moe_gate_up",
    )(*sched, xs, w_gu, b_gu3, b_gu3)

    return pl.pallas_call(
        functools.partial(_down_kernel, second_off=half_d // tn2), out_shape=jax.ShapeDtypeStruct((n_rows, half_d), U32),
        grid_spec=grid_spec(f, f, tn2, half_d // tn2),
        compiler_params=_cparams(("arbitrary", "arbitrary"), 4 * f * tn2 * 4 + 2 * f * tn2 * 2 + 8 * tb * f * 2),
        name="moe_down",
    )(*sched, act, w_down, b_down3, b_down3)


def _combine_kernel(dest_ref, x1_ref, gt_ref, slab_ref, yb_ref, y_ref, buf, sem, *, tm):
    def row_copy(r, k):
        return pltpu.make_async_copy(yb_ref.at[pl.ds(dest_ref[r * TOP_K + k], 1), :],
                                     buf.at[k, pl.ds(r, 1), :], sem)

    def issue(r, c):
        for k in range(TOP_K):
            row_copy(r, k).start(priority=k % 2)
        return c

    def drain(r, c):
        for k in range(TOP_K):
            row_copy(r, k).wait()
        return c

    lax.fori_loop(0, tm, issue, 0, unroll=DMA_UNROLL)
    lax.fori_loop(0, tm, drain, 0, unroll=DMA_UNROLL)
    slab = slab_ref[...]
    acc_a = None
    for k in range(TOP_K):
        a, b = _unpack_bf16_pairs(buf[k])
        g = slab[:, 2 * TOP_K + k:2 * TOP_K + k + 1]
        acc_a = g * a if acc_a is None else acc_a + g * a
        acc_b = g * b if k == 0 else acc_b + g * b
    moe = jnp.concatenate([acc_a, acc_b], axis=1)
    y_ref[...] = x1_ref[...] + gt_ref[...] * moe


def _combine(x1, gate, slab, dest_flat, yb, *, rows_per_mod, tm):
    n, d = x1.shape
    mod_rows = gate.shape[1]
    return pl.pallas_call(
        functools.partial(_combine_kernel, tm=tm),
        out_shape=jax.ShapeDtypeStruct((n, d), F32),
        grid=(n // tm,),
        in_specs=[pl.BlockSpec((tm * TOP_K,), lambda i: (i,), memory_space=pltpu.SMEM),
                  pl.BlockSpec((tm, d), lambda i: (i, 0)),
                  pl.BlockSpec((None, mod_rows, d), lambda i: (i * tm // rows_per_mod, 0, 0)),
                  pl.BlockSpec((tm, LANES), lambda i: (i, 0)),
                  pl.BlockSpec(memory_space=pl.ANY)],
        out_specs=pl.BlockSpec((tm, d), lambda i: (i, 0)),
        scratch_shapes=[pltpu.VMEM((TOP_K, tm, d // 2), U32), pltpu.SemaphoreType.DMA(())],
        compiler_params=_cparams(("arbitrary",), 8 * tm * d * 4),
        name="moe_combine",
    )(dest_flat, x1, gate, slab, yb)


def _rope_table(pos):
    inv = ROPE_THETA ** (-jnp.arange(0, MLA_ROPE, 2, dtype=F32) / MLA_ROPE)
    ang = pos.astype(F32)[:, None] * inv[None, :]
    c, s = jnp.cos(ang), jnp.sin(ang)
    return jnp.concatenate([c, c, -s, s], axis=-1)


def _swap_halves(a):
    h = a.shape[-1] // 2
    return jnp.concatenate([a[..., h:], a[..., :h]], axis=-1)


def kernel(x_prompt, x_sample, cache_dk, cache_dv, cache_ckv, cache_kpe, page_table, c_prompt, c_sample, norm1_g, norm2_g, w_ada, b_ada, w_in, diff_qn_g, diff_kn_g, diff_lambda, diff_subln_g, mla_qa_g, w_qb, mla_qn_g, mla_kva_g, mla_kpe_g, w_uk, w_uv, w_out, w_router, b_router, w_gu, b_gu, w_down, b_down):
    depth = w_in.shape[0]
    assert depth == 1, "single-layer trunk"
    batch, seq, d = x_prompt.shape
    n_seq, dec_seq, _ = x_sample.shape
    assert dec_seq == 1
    n_pool = cache_ckv.shape[1]
    n_pages = page_table.shape[1]
    past_len = n_pages * PAGE_SIZE
    n_p = batch * seq
    n_all = n_p + n_seq
    lam_init = 0.8 - 0.6 * math.exp(-0.3 * 0)
    assert seq % TM_PROJ == 0 and seq % TQ_MLA == 0 and seq % TK_ATTN == 0 and n_p % TM_TOK == 0 and n_seq % TM_TOK == 0
    assert TB_MOE % ZERO_CHUNK == 0

    w_in0 = w_in[0]
    w_in_ext = jnp.concatenate([w_in0, _swap_halves(w_in0[:, O_KPE:])], axis=1).astype(BF16)
    gmat = (jnp.arange(LANES)[:, None] // DIFF_DH == jnp.arange(LANES)[None, :] // DIFF_DH).astype(BF16)
    gdq = (jnp.tile(diff_qn_g[0].reshape(-1), DIFF_HEADS) * (DIFF_DH ** -0.5)).reshape(1, -1)
    gdk = jnp.tile(diff_kn_g[0].reshape(-1), DIFF_KV_HEADS).reshape(1, -1)
    gkpe = jnp.concatenate([mla_kpe_g[0], _swap_halves(mla_kpe_g[0])]).reshape(1, -1)
    qscale = MLA_QK_DIM ** -0.5
    gqn = (mla_qn_g[0, :MLA_NOPE] * qscale).reshape(1, -1)
    gq_pe = mla_qn_g[0, MLA_NOPE:]
    gqp = (jnp.concatenate([gq_pe, _swap_halves(gq_pe)]) * qscale).reshape(1, -1)
    wqb3 = w_qb[0].reshape(Q_LORA, MLA_HEADS, MLA_QK_DIM)
    wqb_pe = wqb3[:, :, MLA_NOPE:]
    wqb_ext = jnp.concatenate([wqb3[:, :, :MLA_NOPE].reshape(Q_LORA, -1),
                               jnp.concatenate([wqb_pe, _swap_halves(wqb_pe)], axis=-1).reshape(Q_LORA, -1)],
                              axis=1).astype(BF16)
    w_kvup = jnp.concatenate([jnp.transpose(w_uk[0], (1, 0, 2)).reshape(KV_LORA, -1),
                              jnp.transpose(w_uv[0], (1, 0, 2)).reshape(KV_LORA, -1)], axis=1).astype(BF16)
    w_uk_t = jnp.transpose(w_uk[0], (0, 2, 1)).astype(BF16)
    w_uv_bf = w_uv[0].astype(BF16)
    proj_consts = [norm1_g[0].reshape(1, -1), w_in_ext, gmat, gdq, gdk, mla_qa_g[0].reshape(1, -1),
                   mla_kva_g[0].reshape(1, -1), gkpe, wqb_ext, gqn, gqp]
    lam_v = diff_lambda[0]
    gsub = diff_subln_g[0].reshape(1, -1)

    mod = _modulation(jnp.concatenate([c_prompt, c_sample], axis=0), w_ada[0], b_ada[0])
    mod_p = mod[:batch].reshape(batch, 6, 1, d)
    mod_s = mod[batch:].reshape(n_seq, 6, d)
    sh1_p, sc1_p, gt1_p, sh2_p, sc2_p, gt2_p = [mod_p[:, t] for t in range(6)]
    sh1_s, sc1_s, gt1_s, sh2_s, sc2_s, gt2_s = [mod_s[:, t][None] for t in range(6)]

    xp2 = x_prompt.reshape(n_p, d)
    xs2 = x_sample.reshape(n_seq, d)
    tab_p = _rope_table(jnp.arange(seq, dtype=I32))
    tab_s = _rope_table(jnp.full((1,), past_len, I32))
    (qds, dk_p, dv_p, ckv_p, kpe_p, qmla, kmla, vmla) = _project(
        xp2, sh1_p, sc1_p, tab_p, proj_consts, w_kvup, sample=False, rows_per_mod=seq, tm=TM_PROJ)
    (qd_s, dk_s, dv_s, ckv_s, kpe_s, qlat_s, qpe_s) = _project(
        xs2, sh1_s, sc1_s, tab_s, proj_consts, w_uk_t, sample=True, rows_per_mod=n_seq, tm=n_seq)

    od_p = _diff_prompt_attention(qds, dk_p, dv_p, lam_v, gsub, batch=batch, seq=seq, lam_init=lam_init)
    om_p = _mla_prompt_attention(qmla, kmla, vmla, batch=batch, seq=seq)

    c_ckv = cache_ckv.reshape(n_pool, PAGE_SIZE, KV_LORA)
    c_kpe = jnp.transpose(cache_kpe, (0, 1, 3, 2)).reshape(n_pool, MLA_ROPE, PAGE_SIZE)
    c_dk = jnp.transpose(cache_dk, (0, 1, 3, 4, 5, 2)).reshape(n_pool, DIFF_K_COLS, PAGE_SIZE)
    c_dv = cache_dv.reshape(n_pool, PAGE_SIZE * DIFF_KV_HEADS, HEAD_DIM)
    q5 = qd_s.reshape(n_seq, DIFF_KV_HEADS, DIFF_GROUP, 2, DIFF_DH)
    eye_kv = jnp.eye(DIFF_KV_HEADS, dtype=F32)
    eye_m = jnp.eye(2, dtype=F32)
    qdm = jnp.einsum("skgmd,kK,mM->skmgKMd", q5, eye_kv, eye_m).reshape(n_seq, 2 * DIFF_HEADS, DIFF_K_COLS)
    head_id = jnp.arange(2 * DIFF_HEADS) // (2 * DIFF_GROUP) * DIFF_GROUP + jnp.arange(2 * DIFF_HEADS) % DIFF_GROUP
    slopes = jnp.broadcast_to((2.0 ** (-(head_id + 1).astype(F32)))[:, None], (2 * DIFF_HEADS, LANES))
    qpe3 = qpe_s.reshape(n_seq, MLA_HEADS, LANES)[:, :, :MLA_ROPE]
    olat_s, od_s = _decode_attention(
        page_table, lam_v, gsub, slopes, qlat_s.reshape(n_seq, MLA_HEADS, KV_LORA), qpe3,
        ckv_s.reshape(n_seq, 1, KV_LORA), kpe_s.reshape(n_seq, 1, MLA_ROPE), qdm,
        dk_s.reshape(n_seq, 1, DIFF_K_COLS), dv_s.reshape(n_seq, 1, DIFF_V_COLS),
        c_ckv, c_kpe, c_dk, c_dv, lam_init=lam_init)
    om_s = _latent_out(olat_s.reshape(n_seq, MLA_HEADS * KV_LORA), w_uv_bf)
    od_s = od_s.reshape(n_seq, DIFF_HEADS * HEAD_DIM).astype(BF16)

    tri = (jnp.arange(TM_PROJ)[:, None] > jnp.arange(TM_PROJ)[None, :]).astype(BF16)
    wr_hi = w_router[0].astype(BF16)
    wr_lo = (w_router[0] - wr_hi.astype(F32)).astype(BF16)
    wr_split = jnp.concatenate([jnp.concatenate([wr_hi, wr_lo], axis=1),
                                jnp.concatenate([wr_hi, jnp.zeros_like(wr_lo)], axis=1)], axis=0)
    out_consts = [norm2_g[0].reshape(1, -1), w_out[0].astype(BF16), wr_split, b_router[0].reshape(1, -1)]
    zero_counts = jnp.zeros((1, N_EXPERTS), F32)
    x1_p, h2_p, slab_p, counts_p = _out_project(
        od_p, om_p, xp2, gt1_p, sh2_p, sc2_p, out_consts + [tri], zero_counts,
        rows_per_mod=seq, tm=TM_PROJ, name="out_proj_router_prompt")
    x1_s, h2_s, slab_s, counts = _out_project(
        od_s, om_s, xs2, gt1_s, sh2_s, sc2_s, out_consts + [tri[:n_seq, :n_seq]], counts_p,
        rows_per_mod=n_seq, tm=n_seq, name="out_proj_router_sample")

    tb = TB_MOE
    n_blk = (n_all * TOP_K + N_EXPERTS * (tb - 1)) // tb + 1
    cnt = counts[0].astype(I32)
    padded = (cnt + tb - 1) // tb * tb
    pad_end = jnp.cumsum(padded)
    pad_start = pad_end - padded
    slab = jnp.concatenate([slab_p, slab_s], axis=0)
    e_idx = slab[:, :TOP_K].astype(I32)
    rank = slab[:, TOP_K:2 * TOP_K].astype(I32)
    experts = jnp.arange(N_EXPERTS, dtype=I32)
    start_of = jnp.sum(jnp.where(e_idx[..., None] == experts, pad_start, 0), axis=-1)
    dest = (start_of + rank).reshape(-1)
    n_used = (pad_end[-1] // tb).astype(I32).reshape(1)
    blk_start = jnp.arange(n_blk, dtype=I32) * tb
    block_exp = jnp.minimum(jnp.sum((pad_end[None, :] <= blk_start[:, None]).astype(I32), axis=1), N_EXPERTS - 1)

    n_rows = n_blk * tb
    zero_lo = jnp.concatenate([pad_start + cnt, pad_end[-1:]]).astype(I32)
    zero_hi = jnp.concatenate([pad_end, jnp.full((1,), n_rows, I32)]).astype(I32)
    xs = _dispatch(h2_p, h2_s, dest, zero_lo, zero_hi, n_rows)
    fill_end = jnp.sum(jnp.where(block_exp[:, None] == experts, pad_start + cnt, 0), axis=-1)
    rows_filled = jnp.clip(fill_end - blk_start, 0, tb).astype(I32)
    has_rows = cnt > 0
    run_of_expert = jnp.cumsum(has_rows.astype(I32)) - 1
    own_or_none = jnp.where(has_rows, experts, N_EXPERTS)
    next_of_expert = jnp.concatenate([lax.cummin(own_or_none[::-1])[::-1][1:], jnp.full((1,), N_EXPERTS, I32)])
    is_exp = block_exp[:, None] == experts
    run_idx = jnp.sum(jnp.where(is_exp, run_of_expert, 0), axis=-1).astype(I32)
    next_exp = jnp.sum(jnp.where(is_exp, next_of_expert, 0), axis=-1).astype(I32)
    meta = jnp.stack([jnp.sum(has_rows.astype(I32)), jnp.min(own_or_none)]).astype(I32)
    sched = (block_exp.astype(I32), n_used, rows_filled, run_idx, next_exp, meta)
    yb = _experts(xs, sched, w_gu[0], b_gu[0], w_down[0], b_down[0])
    y_p = _combine(x1_p, gt2_p, slab_p, dest[:n_p * TOP_K], yb, rows_per_mod=seq, tm=TM_TOK)
    y_s = _combine(x1_s, gt2_s, slab_s, dest[n_p * TOP_K:], yb, rows_per_mod=n_seq, tm=TM_TOK)

    def kv_outputs(dk, dv, ckv, kpe, b, t):
        return (dk.reshape(1, b, t, DIFF_KV_HEADS, 2, DIFF_DH), dv.reshape(1, b, t, DIFF_KV_HEADS, HEAD_DIM),
                ckv.reshape(1, b, t, KV_LORA), kpe.reshape(1, b, t, MLA_ROPE))

    dk_p = jnp.transpose(dk_p.reshape(batch, DIFF_KV_HEADS, 2, DIFF_DH, seq), (0, 4, 1, 2, 3))[None]
    kpe_p = jnp.transpose(kpe_p, (0, 2, 1))[None]
    return ((y_p.reshape(batch, seq, d), y_s.reshape(n_seq, 1, d), dk_p, dv_p.reshape(1, batch, seq, DIFF_KV_HEADS, HEAD_DIM),
             ckv_p.reshape(1, batch, seq, KV_LORA), kpe_p) + kv_outputs(dk_s, dv_s, ckv_s, kpe_s, n_seq, 1))
```
